```python
import math
import jax, jax.numpy as jnp
from jax import lax
import numpy as np

D_MODEL = 1024
BATCH = 8
SEQ = 4096
DEPTH = 4

N_META = 16
LEAD = 128
PAD = LEAD - N_META
EPS = 1e-6

GLA_HEADS = 4
GLA_DK = D_MODEL // 2
GLA_DV = D_MODEL
GLA_HK = GLA_DK // GLA_HEADS
GLA_HV = GLA_DV // GLA_HEADS
GLA_RANK = 16
GLA_NORMALIZER = 16.0
GLA_CHUNK = 64

SWA_HQ = 16
SWA_HKV = 2
SWA_HD = 64
SWA_GRP = SWA_HQ // SWA_HKV
SWA_WINDOW = 128
SWA_BLOCK = 128

SSD_DINNER = 2 * D_MODEL
SSD_HEADDIM = 64
SSD_HEADS = SSD_DINNER // SSD_HEADDIM
SSD_GROUPS = 4
SSD_HPG = SSD_HEADS // SSD_GROUPS
SSD_DSTATE = 128
SSD_CONV = 4
SSD_CHUNK = 128
SSD_XBC = SSD_DINNER + 2 * SSD_GROUPS * SSD_DSTATE

N_BRANCH = 3
D_FF = 4 * D_MODEL

IN_SIZES = (GLA_DK, GLA_DK, GLA_DV, GLA_DV, GLA_RANK,
            SWA_HQ * SWA_HD, SWA_HKV * SWA_HD, SWA_HKV * SWA_HD,
            SSD_DINNER, SSD_XBC, SSD_HEADS,
            N_BRANCH * D_MODEL)
N_IN = sum(IN_SIZES)

kernel_name = "hybrid_gla_swa_ssd_gated_merge"


def _rmsnorm(x, g):
    xf = x.astype(jnp.float32)
    y = xf * lax.rsqrt(jnp.mean(xf * xf, axis=-1, keepdims=True) + EPS)
    return (y * g.astype(jnp.float32)).astype(x.dtype)


def _gla(q, k, v, r, g_lr, w_gate_up, b_gate_up, norm_w):
    f32 = jnp.float32
    bsz, L, _ = q.shape
    n = L // GLA_CHUNK
    gk = jax.nn.log_sigmoid((g_lr @ w_gate_up).astype(f32) + b_gate_up.astype(f32)) / GLA_NORMALIZER

    def chunk(t, hd):
        return t.astype(f32).reshape(bsz, n, GLA_CHUNK, GLA_HEADS, hd)

    qc = chunk(q, GLA_HK) * (GLA_HK ** -0.5)
    kc = chunk(k, GLA_HK)
    vc = chunk(v, GLA_HV)
    G = jnp.cumsum(chunk(gk, GLA_HK), axis=2)
    G_last = G[:, :, -1]
    q_dec = qc * jnp.exp(G)
    k_inv = kc * jnp.exp(-G)
    causal = jnp.tril(jnp.ones((GLA_CHUNK, GLA_CHUNK), dtype=bool))
    attn = jnp.where(causal, jnp.einsum('bnihd,bnjhd->bnhij', q_dec, k_inv), 0.0)
    o_intra = jnp.einsum('bnhij,bnjhv->bnihv', attn, vc)
    k_end = kc * jnp.exp(G_last[:, :, None] - G)
    chunk_kv = jnp.einsum('bnjhd,bnjhv->bnhdv', k_end, vc)
    decay = jnp.exp(G_last)

    def step(S, inp):
        dec, kv = inp
        return S * dec[..., None] + kv, S

    S0 = jnp.zeros((bsz, GLA_HEADS, GLA_HK, GLA_HV), f32)
    _, S_prev = lax.scan(step, S0, (jnp.moveaxis(decay, 1, 0), jnp.moveaxis(chunk_kv, 1, 0)))
    S_prev = jnp.moveaxis(S_prev, 0, 1)
    o_inter = jnp.einsum('bnihd,bnhdv->bnihv', q_dec, S_prev)
    o = (o_intra + o_inter).reshape(bsz, L, GLA_HEADS, GLA_HV)
    o = o * lax.rsqrt(jnp.mean(o * o, axis=-1, keepdims=True) + EPS) * norm_w.astype(f32)
    o = o * jax.nn.silu(r.astype(f32).reshape(bsz, L, GLA_HEADS, GLA_HV))
    return o.reshape(bsz, L, GLA_DV)


def _swa(q, k, v, sinks):
    f32 = jnp.float32
    bsz, L, _ = q.shape
    nb = L // SWA_BLOCK
    qb = q.reshape(bsz, nb, SWA_BLOCK, SWA_HKV, SWA_GRP, SWA_HD)
    kb = k.reshape(bsz, nb, SWA_BLOCK, SWA_HKV, SWA_HD)
    vb = v.reshape(bsz, nb, SWA_BLOCK, SWA_HKV, SWA_HD)

    def with_prev(t):
        prev = jnp.concatenate([jnp.zeros_like(t[:, :1]), t[:, :-1]], axis=1)
        return jnp.concatenate([prev, t], axis=2)

    kw, vw = with_prev(kb), with_prev(vb)
    s = jnp.einsum('bnqhgd,bnkhd->bnhgqk', qb, kw).astype(f32) * (SWA_HD ** -0.5)
    blk = jnp.arange(nb)[:, None] * SWA_BLOCK
    qpos = blk + jnp.arange(SWA_BLOCK)[None]
    kpos = blk - SWA_BLOCK + jnp.arange(2 * SWA_BLOCK)[None]
    dq = qpos[:, :, None] - kpos[:, None, :]
    allowed = (dq >= 0) & (dq < SWA_WINDOW) & (kpos[:, None, :] >= PAD)
    s = jnp.where(allowed[None, :, None, None], s, -jnp.inf)
    sink = jnp.broadcast_to(sinks.astype(f32).reshape(SWA_HKV, SWA_GRP)[None, None, :, :, None, None],
                            s.shape[:-1] + (1,))
    p = jax.nn.softmax(jnp.concatenate([s, sink], axis=-1), axis=-1)[..., :-1]
    o = jnp.einsum('bnhgqk,bnkhd->bnqhgd', p, vw)
    return o.reshape(bsz, L, SWA_HQ * SWA_HD)


def _ssd(z, xbc, dt_raw, conv_w, conv_b, dt_bias, A_log, D_skip, norm_w, valid):
    f32 = jnp.float32
    bsz, L, _ = z.shape
    n = L // SSD_CHUNK
    xbc = lax.conv_general_dilated(xbc, conv_w[:, None, :], window_strides=(1,),
                                   padding=[(SSD_CONV - 1, 0)],
                                   dimension_numbers=('NWC', 'WIO', 'NWC'),
                                   feature_group_count=SSD_XBC) + conv_b
    xbc = jax.nn.silu(xbc.astype(f32))
    xs = xbc[..., :SSD_DINNER]
    Bm = xbc[..., SSD_DINNER:SSD_DINNER + SSD_GROUPS * SSD_DSTATE]
    Cm = xbc[..., SSD_DINNER + SSD_GROUPS * SSD_DSTATE:]
    dt = jax.nn.softplus(dt_raw.astype(f32) + dt_bias.astype(f32)) * valid[None, :, None].astype(f32)
    A = -jnp.exp(A_log.astype(f32)).reshape(SSD_GROUPS, SSD_HPG)

    xc = xs.reshape(bsz, n, SSD_CHUNK, SSD_GROUPS, SSD_HPG, SSD_HEADDIM)
    Bc = Bm.reshape(bsz, n, SSD_CHUNK, SSD_GROUPS, SSD_DSTATE)
    Cc = Cm.reshape(bsz, n, SSD_CHUNK, SSD_GROUPS, SSD_DSTATE)
    dtc = dt.reshape(bsz, n, SSD_CHUNK, SSD_GROUPS, SSD_HPG)
    a_cs = jnp.cumsum(dtc * A, axis=2)
    xdt = xc * dtc[..., None]
    causal = jnp.tril(jnp.ones((SSD_CHUNK, SSD_CHUNK), dtype=bool))
    seg = a_cs[:, :, :, None] - a_cs[:, :, None, :]
    Lmat = jnp.exp(jnp.where(causal[:, :, None, None], seg, -jnp.inf))
    CB = jnp.einsum('bnlgd,bnsgd->bnlsg', Cc, Bc)
    y_diag = jnp.einsum('bnlsgh,bnsghp->bnlghp', CB[..., None] * Lmat, xdt)
    decay_end = jnp.exp(a_cs[:, :, -1:] - a_cs)
    states = jnp.einsum('bnsgd,bnsghp->bnghpd', Bc, xdt * decay_end[..., None])
    chunk_decay = jnp.exp(a_cs[:, :, -1])

    def step(S, inp):
        dec, st = inp
        return S * dec[..., None, None] + st, S

    S0 = jnp.zeros((bsz, SSD_GROUPS, SSD_HPG, SSD_HEADDIM, SSD_DSTATE), f32)
    _, S_prev = lax.scan(step, S0, (jnp.moveaxis(chunk_decay, 1, 0), jnp.moveaxis(states, 1, 0)))
    S_prev = jnp.moveaxis(S_prev, 0, 1)
    y_off = jnp.einsum('bnlgd,bnghpd->bnlghp', Cc, S_prev) * jnp.exp(a_cs)[..., None]
    y = y_diag + y_off + D_skip.astype(f32).reshape(SSD_GROUPS, SSD_HPG)[:, :, None] * xc
    y = y.reshape(bsz, L, SSD_DINNER) * jax.nn.silu(z.astype(f32))
    y = y.reshape(bsz, L, SSD_GROUPS, SSD_DINNER // SSD_GROUPS)
    y = y * lax.rsqrt(jnp.mean(y * y, axis=-1, keepdims=True) + EPS)
    return y.reshape(bsz, L, SSD_DINNER) * norm_w.astype(f32)


def _layer(x, valid, norm_mix, w_in, b_gate, gla_w_gate_up, gla_b_gate_up, gla_norm, swa_sinks,
           ssd_conv_w, ssd_conv_b, ssd_dt_bias, ssd_A_log, ssd_D, ssd_norm,
           w_branch_gla, w_branch_swa, w_branch_ssd, w_out, norm_mlp, w_mlp_in, w_mlp_out):
    bsz, L, _ = x.shape
    vmask = valid[None, :, None]
    h = _rmsnorm(x, norm_mix)
    proj = h @ w_in
    split_pts = [int(v) for v in np.cumsum(IN_SIZES)[:-1]]
    (g_q, g_k, g_v, g_r, g_lr, s_q, s_k, s_v, m_z, m_xbc, m_dt, gate_logits) = jnp.split(proj, split_pts, axis=-1)
    y_a = _gla(g_q, g_k, g_v, g_r, g_lr, gla_w_gate_up, gla_b_gate_up, gla_norm) @ w_branch_gla
    y_b = _swa(s_q, s_k, s_v, swa_sinks) @ w_branch_swa
    y_c = _ssd(m_z, m_xbc, m_dt, ssd_conv_w, ssd_conv_b, ssd_dt_bias, ssd_A_log, ssd_D, ssd_norm, valid) @ w_branch_ssd
    gates = jax.nn.sigmoid(gate_logits.astype(jnp.float32) + b_gate.astype(jnp.float32))
    gates = gates.reshape(bsz, L, N_BRANCH, D_MODEL)
    merged = gates[:, :, 0] * y_a + gates[:, :, 1] * y_b + gates[:, :, 2] * y_c
    x = x + (merged.astype(x.dtype) @ w_out) * vmask
    h2 = _rmsnorm(x, norm_mlp)
    x = x + (jnp.square(jax.nn.relu(h2 @ w_mlp_in)) @ w_mlp_out) * vmask
    return x


def setup_inputs(seed: int = 0) -> dict:
    key = jax.random.key(seed)
    ks = iter(jax.random.split(key, 32))

    def nrm(shape, scale):
        return jax.random.normal(next(ks), shape, jnp.float32) * scale

    x = nrm((BATCH, SEQ, D_MODEL), 1.0)
    meta_tokens = nrm((N_META, D_MODEL), 1.0)
    norm_mix = 1.0 + nrm((DEPTH, D_MODEL), 0.02)
    w_in = nrm((DEPTH, D_MODEL, N_IN), D_MODEL ** -0.5)
    b_gate = nrm((DEPTH, N_BRANCH * D_MODEL), 0.1)
    gla_w_gate_up = nrm((DEPTH, GLA_RANK, GLA_DK), GLA_RANK ** -0.5)
    gla_b_gate_up = nrm((DEPTH, GLA_DK), 0.1)
    gla_norm = 1.0 + nrm((DEPTH, GLA_HV), 0.02)
    swa_sinks = nrm((DEPTH, SWA_HQ), 0.5)
    ssd_conv_w = nrm((DEPTH, SSD_CONV, SSD_XBC), SSD_CONV ** -0.5)
    ssd_conv_b = nrm((DEPTH, SSD_XBC), 0.02)
    dt0 = jnp.exp(jax.random.uniform(next(ks), (DEPTH, SSD_HEADS), jnp.float32,
                                     minval=math.log(1e-3), maxval=math.log(1e-1)))
    ssd_dt_bias = dt0 + jnp.log(-jnp.expm1(-dt0))
    ssd_A_log = jnp.log(jax.random.uniform(next(ks), (DEPTH, SSD_HEADS), jnp.float32, minval=1.0, maxval=16.0))
    ssd_D = 1.0 + nrm((DEPTH, SSD_HEADS), 0.1)
    ssd_norm = 1.0 + nrm((DEPTH, SSD_DINNER), 0.02)
    w_branch_gla = nrm((DEPTH, GLA_DV, D_MODEL), GLA_DV ** -0.5)
    w_branch_swa = nrm((DEPTH, SWA_HQ * SWA_HD, D_MODEL), (SWA_HQ * SWA_HD) ** -0.5)
    w_branch_ssd = nrm((DEPTH, SSD_DINNER, D_MODEL), SSD_DINNER ** -0.5)
    w_out = nrm((DEPTH, D_MODEL, D_MODEL), D_MODEL ** -0.5)
    norm_mlp = 1.0 + nrm((DEPTH, D_MODEL), 0.02)
    w_mlp_in = nrm((DEPTH, D_MODEL, D_FF), D_MODEL ** -0.5)
    w_mlp_out = nrm((DEPTH, D_FF, D_MODEL), D_FF ** -0.5)
    final_norm = 1.0 + nrm((D_MODEL,), 0.02)
    return {"x": x, "meta_tokens": meta_tokens, "norm_mix": norm_mix, "w_in": w_in, "b_gate": b_gate,
            "gla_w_gate_up": gla_w_gate_up, "gla_b_gate_up": gla_b_gate_up, "gla_norm": gla_norm,
            "swa_sinks": swa_sinks, "ssd_conv_w": ssd_conv_w, "ssd_conv_b": ssd_conv_b,
            "ssd_dt_bias": ssd_dt_bias, "ssd_A_log": ssd_A_log, "ssd_D": ssd_D, "ssd_norm": ssd_norm,
            "w_branch_gla": w_branch_gla, "w_branch_swa": w_branch_swa, "w_branch_ssd": w_branch_ssd,
            "w_out": w_out, "norm_mlp": norm_mlp, "w_mlp_in": w_mlp_in, "w_mlp_out": w_mlp_out,
            "final_norm": final_norm}


def reference(x, meta_tokens, norm_mix, w_in, b_gate, gla_w_gate_up, gla_b_gate_up, gla_norm, swa_sinks,
              ssd_conv_w, ssd_conv_b, ssd_dt_bias, ssd_A_log, ssd_D, ssd_norm,
              w_branch_gla, w_branch_swa, w_branch_ssd, w_out, norm_mlp, w_mlp_in, w_mlp_out, final_norm):
    bsz, seq, d = x.shape
    h = jnp.concatenate([jnp.zeros((bsz, PAD, d), x.dtype),
                         jnp.broadcast_to(meta_tokens.astype(x.dtype)[None], (bsz, N_META, d)),
                         x], axis=1)
    valid = (jnp.arange(LEAD + seq) >= PAD).astype(x.dtype)
    for i in range(DEPTH):
        h = _layer(h, valid, norm_mix[i], w_in[i], b_gate[i], gla_w_gate_up[i], gla_b_gate_up[i], gla_norm[i],
                   swa_sinks[i], ssd_conv_w[i], ssd_conv_b[i], ssd_dt_bias[i], ssd_A_log[i], ssd_D[i], ssd_norm[i],
                   w_branch_gla[i], w_branch_swa[i], w_branch_ssd[i], w_out[i], norm_mlp[i], w_mlp_in[i], w_mlp_out[i])
    h = _rmsnorm(h, final_norm)
    return h[:, LEAD:, :]
```

```python
import functools

import jax
import jax.numpy as jnp
from jax import lax
from jax.experimental import pallas as pl
from jax.experimental.pallas import tpu as pltpu

F32 = jnp.float32
BF16 = jnp.bfloat16

D_MODEL = 1024
DEPTH = 4
N_META = 16
LEAD = 128
PAD = LEAD - N_META
EPS = 1e-6

GLA_HEADS = 4
GLA_DK = D_MODEL // 2
GLA_DV = D_MODEL
GLA_HK = GLA_DK // GLA_HEADS
GLA_HV = GLA_DV // GLA_HEADS
GLA_RANK = 16
GLA_NORMALIZER = 16.0
GLA_CHUNK = 64

SWA_HQ = 16
SWA_HKV = 2
SWA_HD = 64
SWA_BLOCK = 128

SSD_DINNER = 2 * D_MODEL
SSD_HEADDIM = 64
SSD_HEADS = SSD_DINNER // SSD_HEADDIM
SSD_GROUPS = 4
SSD_DSTATE = 128
SSD_CONV = 4
SSD_CHUNK = 128
SSD_BC = SSD_GROUPS * SSD_DSTATE
SSD_XBC = SSD_DINNER + 2 * SSD_BC
SSD_GROUP_WIDTH = SSD_DINNER // SSD_GROUPS

N_BRANCH = 3
D_FF = 4 * D_MODEL

LANES = 128
SUBLANES = 8

SEG_GATE, SEG_XBC, SEG_Z, SEG_GV, SEG_GR, SEG_SQ, SEG_GQ, SEG_GK, SEG_SK, SEG_SV = range(10)
_SEG_WIDTHS = (N_BRANCH * D_MODEL, SSD_XBC, SSD_DINNER, GLA_DV, GLA_DV,
               SWA_HQ * SWA_HD, GLA_DK, GLA_DK, SWA_HKV * SWA_HD, SWA_HKV * SWA_HD)
_SEG_STARTS = tuple(sum(_SEG_WIDTHS[:i]) for i in range(len(_SEG_WIDTHS)))
N_BIG = sum(_SEG_WIDTHS)
assert all(s % w == 0 for s, w in zip(_SEG_STARTS, _SEG_WIDTHS))
SMALL_DT = 0
SMALL_LR = SSD_HEADS
N_SMALL = LANES

PROJ_TM = 1024
PROJ_TN = 1792
ROW_TILE = 384
GLA_ROWS = 128
VMEM_LIMIT = 56 * 1024 * 1024


def _seg_block(seg):
    return _SEG_STARTS[seg] // _SEG_WIDTHS[seg]


def _nt(a, b):
    return lax.dot_general(a, b, (((1,), (1,)), ((), ())), preferred_element_type=F32)


def _tn(a, b):
    return lax.dot_general(a, b, (((0,), (0,)), ((), ())), preferred_element_type=F32)


def _split_dot(m01, x):
    hi = x.astype(BF16)
    r1 = x - hi.astype(F32)
    mid = r1.astype(BF16)
    lo = (r1 - mid.astype(F32)).astype(BF16)
    return (jnp.dot(m01, hi, preferred_element_type=F32)
            + jnp.dot(m01, mid, preferred_element_type=F32)
            + jnp.dot(m01, lo, preferred_element_type=F32))


def _tril_ones(n):
    r = lax.broadcasted_iota(jnp.int32, (n, n), 0)
    c = lax.broadcasted_iota(jnp.int32, (n, n), 1)
    return r >= c


def _softplus(x):
    return jnp.maximum(x, 0.0) + jnp.log1p(jnp.exp(-jnp.abs(x)))


def _silu(x):
    return x * jax.nn.sigmoid(x)


def _rms(x, g):
    return x * lax.rsqrt(jnp.mean(x * x, axis=-1, keepdims=True) + EPS) * g


def _norm_kernel(x_ref, g_ref, o_ref):
    o_ref[0] = _rms(x_ref[0], g_ref[...]).astype(o_ref.dtype)


def _norm(x, g):
    bsz, L, d = x.shape
    return pl.pallas_call(
        _norm_kernel,
        grid=(bsz, L // ROW_TILE),
        in_specs=[pl.BlockSpec((1, ROW_TILE, d), lambda b, j: (b, j, 0)),
                  pl.BlockSpec((1, d), lambda b, j: (0, 0))],
        out_specs=pl.BlockSpec((1, ROW_TILE, d), lambda b, j: (b, j, 0)),
        out_shape=jax.ShapeDtypeStruct((bsz, L, d), BF16),
        name="first_norm",
    )(x, g)


def _inproj_kernel(h_ref, w_ref, ws_ref, o_ref, os_ref):
    o_ref[...] = jnp.dot(h_ref[...], w_ref[...], preferred_element_type=F32).astype(o_ref.dtype)

    @pl.when(pl.program_id(1) == 0)
    def _():
        os_ref[...] = jnp.dot(h_ref[...], ws_ref[...], preferred_element_type=F32)


def _inproj(hn, w_big, w_small):
    T, d = hn.shape
    return pl.pallas_call(
        _inproj_kernel,
        grid=(T // PROJ_TM, N_BIG // PROJ_TN),
        in_specs=[pl.BlockSpec((PROJ_TM, d), lambda i, j: (i, 0)),
                  pl.BlockSpec((d, PROJ_TN), lambda i, j: (0, j)),
                  pl.BlockSpec((d, N_SMALL), lambda i, j: (0, 0))],
        out_specs=[pl.BlockSpec((PROJ_TM, PROJ_TN), lambda i, j: (i, j)),
                   pl.BlockSpec((PROJ_TM, N_SMALL), lambda i, j: (i, 0))],
        out_shape=[jax.ShapeDtypeStruct((T, N_BIG), BF16),
                   jax.ShapeDtypeStruct((T, N_SMALL), F32)],
        compiler_params=pltpu.CompilerParams(
            dimension_semantics=("arbitrary", "arbitrary"), vmem_limit_bytes=VMEM_LIMIT),
        name="in_proj",
    )(hn, w_big, w_small)


def _gla_kernel(q_ref, k_ref, v_ref, r_ref, sm_ref, wg_ref, bg_ref, nw_ref, o_ref, st_ref):
    @pl.when(pl.program_id(1) == 0)
    def _():
        st_ref[...] = jnp.zeros_like(st_ref)

    logits = jnp.dot(sm_ref[0].astype(BF16), wg_ref[...], preferred_element_type=F32) + bg_ref[...]
    gk_all = (jnp.minimum(logits, 0.0) - jnp.log1p(jnp.exp(-jnp.abs(logits)))) * (1.0 / GLA_NORMALIZER)
    causal = _tril_ones(GLA_CHUNK)
    tril = causal.astype(BF16)
    for c in range(GLA_ROWS // GLA_CHUNK):
        rows = slice(c * GLA_CHUNK, (c + 1) * GLA_CHUNK)
        g_all = _split_dot(tril, gk_all[rows])
        for h in range(GLA_HEADS):
            hk = slice(h * GLA_HK, (h + 1) * GLA_HK)
            hv = slice(h * GLA_HV, (h + 1) * GLA_HV)
            g = g_all[:, hk]
            g_last = g[GLA_CHUNK - 1:GLA_CHUNK, :]
            q = q_ref[0, rows, hk].astype(F32) * (GLA_HK ** -0.5)
            k = k_ref[0, rows, hk].astype(F32)
            v = v_ref[0, rows, hv]
            q_dec = (q * jnp.exp(g)).astype(BF16)
            k_inv = (k * jnp.exp(-g)).astype(BF16)
            k_end = (k * jnp.exp(g_last - g)).astype(BF16)
            attn = jnp.where(causal, _nt(q_dec, k_inv), 0.0).astype(BF16)
            state_t = st_ref[h]
            o = jnp.dot(attn, v, preferred_element_type=F32) + _nt(q_dec, state_t.astype(BF16))
            st_ref[h] = state_t * jnp.exp(g_last) + _tn(v, k_end)
            o = o * lax.rsqrt(jnp.mean(o * o, axis=-1, keepdims=True) + EPS) * nw_ref[...]
            o = o * _silu(r_ref[0, rows, hv].astype(F32))
            o_ref[0, rows, hv] = o.astype(o_ref.dtype)


def _gla(proj, small, wg, bg, nw):
    bsz, L, _ = proj.shape
    row = lambda seg: (lambda b, n: (b, n, _seg_block(seg)))
    const = lambda b, n: (0, 0)
    return pl.pallas_call(
        _gla_kernel,
        grid=(bsz, L // GLA_ROWS),
        in_specs=[pl.BlockSpec((1, GLA_ROWS, GLA_DK), row(SEG_GQ)),
                  pl.BlockSpec((1, GLA_ROWS, GLA_DK), row(SEG_GK)),
                  pl.BlockSpec((1, GLA_ROWS, GLA_DV), row(SEG_GV)),
                  pl.BlockSpec((1, GLA_ROWS, GLA_DV), row(SEG_GR)),
                  pl.BlockSpec((1, GLA_ROWS, N_SMALL), lambda b, n: (b, n, 0)),
                  pl.BlockSpec((N_SMALL, GLA_DK), const),
                  pl.BlockSpec((1, GLA_DK), const),
                  pl.BlockSpec((1, GLA_HV), const)],
        out_specs=pl.BlockSpec((1, GLA_ROWS, GLA_DV), lambda b, n: (b, n, 0)),
        out_shape=jax.ShapeDtypeStruct((bsz, L, GLA_DV), BF16),
        scratch_shapes=[pltpu.VMEM((GLA_HEADS, GLA_HV, GLA_HK), F32)],
        compiler_params=pltpu.CompilerParams(
            dimension_semantics=("arbitrary", "arbitrary"), vmem_limit_bytes=VMEM_LIMIT),
        name="gla",
    )(proj, proj, proj, proj, small, wg, bg, nw)


def _swa_kernel(q_ref, kp_ref, kc_ref, vp_ref, vc_ref, sink_ref, o_ref):
    n = pl.program_id(1)
    nk = 2 * SWA_BLOCK
    low_q = lax.broadcasted_iota(jnp.int32, (SWA_BLOCK, LANES), 1) < SWA_HD
    low_k = lax.broadcasted_iota(jnp.int32, (nk, LANES), 1) < SWA_HD

    def per_kv_head(prev_ref, cur_ref):
        both = jnp.concatenate([prev_ref[0], cur_ref[0]], axis=0).astype(F32)
        swapped = pltpu.roll(both, SWA_HD, axis=1)
        return (jnp.where(low_k, both, swapped).astype(BF16),
                jnp.where(low_k, swapped, both).astype(BF16))

    k_ext = per_kv_head(kp_ref, kc_ref)
    v_ext = per_kv_head(vp_ref, vc_ref)
    qi = lax.broadcasted_iota(jnp.int32, (SWA_BLOCK, nk), 0)
    kk = lax.broadcasted_iota(jnp.int32, (SWA_BLOCK, nk), 1)
    allowed = (kk > qi) & (kk <= qi + SWA_BLOCK) & (kk >= PAD + SWA_BLOCK - SWA_BLOCK * n)
    heads_per_kv = SWA_HQ // SWA_HKV
    for j in range(SWA_HQ // 2):
        hkv = (2 * j) // heads_per_kv
        cols = slice(j * LANES, (j + 1) * LANES)
        q_pair = q_ref[0, :, cols]
        halves = []
        for half in range(2):
            q_half = jnp.where(low_q if half == 0 else ~low_q, q_pair, jnp.zeros_like(q_pair))
            s = _nt(q_half, k_ext[hkv]) * (SWA_HD ** -0.5)
            s = jnp.where(allowed, s, -jnp.inf)
            sink = sink_ref[2 * j + half]
            m = jnp.maximum(jnp.max(s, axis=-1, keepdims=True), sink)
            p = jnp.exp(s - m)
            den = jnp.sum(p, axis=-1, keepdims=True) + jnp.exp(sink - m)
            o = jnp.dot(p.astype(BF16), v_ext[hkv], preferred_element_type=F32)
            halves.append(o * (1.0 / den))
        o_ref[0, :, cols] = jnp.where(low_q, halves[0], halves[1]).astype(o_ref.dtype)


def _swa(proj, sinks):
    bsz, L, _ = proj.shape
    kvw = SWA_HKV * SWA_HD
    cur = lambda seg: (lambda b, n: (b, n, _seg_block(seg)))
    prev = lambda seg: (lambda b, n: (b, jnp.maximum(n - 1, 0), _seg_block(seg)))
    return pl.pallas_call(
        _swa_kernel,
        grid=(bsz, L // SWA_BLOCK),
        in_specs=[pl.BlockSpec((1, SWA_BLOCK, SWA_HQ * SWA_HD), cur(SEG_SQ)),
                  pl.BlockSpec((1, SWA_BLOCK, kvw), prev(SEG_SK)),
                  pl.BlockSpec((1, SWA_BLOCK, kvw), cur(SEG_SK)),
                  pl.BlockSpec((1, SWA_BLOCK, kvw), prev(SEG_SV)),
                  pl.BlockSpec((1, SWA_BLOCK, kvw), cur(SEG_SV)),
                  pl.BlockSpec(memory_space=pltpu.SMEM)],
        out_specs=pl.BlockSpec((1, SWA_BLOCK, SWA_HQ * SWA_HD), lambda b, n: (b, n, 0)),
        out_shape=jax.ShapeDtypeStruct((bsz, L, SWA_HQ * SWA_HD), BF16),
        compiler_params=pltpu.CompilerParams(
            dimension_semantics=("arbitrary", "arbitrary"), vmem_limit_bytes=VMEM_LIMIT),
        name="swa",
    )(proj, proj, proj, proj, proj, sinks)


def _ssd_kernel(z_ref, xbc_ref, sm_ref, cw_ref, cb_ref, dtb_ref, alog_ref, dskip_ref, nw_ref,
                o_ref, ext_ref, xs_ref, y_ref, st_ref):
    n = pl.program_id(1)
    rows = SSD_CHUNK
    tail = SUBLANES

    @pl.when(n == 0)
    def _():
        ext_ref[0:tail, :] = jnp.zeros((tail, SSD_XBC), F32)
        st_ref[...] = jnp.zeros_like(st_ref)

    ext_ref[tail:tail + rows, :] = xbc_ref[0].astype(F32)
    col_chunk = 512
    bc_parts = []
    for c in range(SSD_XBC // col_chunk):
        cols = slice(c * col_chunk, (c + 1) * col_chunk)
        acc = cb_ref[:, cols]
        for tap in range(SSD_CONV):
            acc = acc + cw_ref[tap:tap + 1, cols] * ext_ref[tail - (SSD_CONV - 1) + tap:
                                                           tail - (SSD_CONV - 1) + tap + rows, cols]
        act = _silu(acc)
        if c < SSD_DINNER // col_chunk:
            xs_ref[:, cols] = act
        else:
            bc_parts.append(act.astype(BF16))
    ext_ref[0:tail, :] = ext_ref[rows:rows + tail, :]
    b_all, c_all = bc_parts

    lane = lax.broadcasted_iota(jnp.int32, (1, LANES), 1)
    pos = n * rows + lax.broadcasted_iota(jnp.int32, (rows, 1), 0)
    valid = (pos >= PAD).astype(F32)
    dt = _softplus(sm_ref[0] + dtb_ref[...]) * valid
    a_coef = jnp.where(lane < SSD_HEADS, -jnp.exp(alog_ref[...]), 0.0)
    causal = _tril_ones(rows)
    a_cs = _split_dot(causal.astype(BF16), dt * a_coef)
    a_last = a_cs[rows - 1:rows, :]
    a_cs_t = a_cs.T
    e_cs = jnp.exp(a_cs)
    d_end = jnp.exp(a_last - a_cs)
    c_decay = jnp.exp(a_last)

    low = lax.broadcasted_iota(jnp.int32, (rows, LANES), 1) < SSD_HEADDIM
    top = lax.broadcasted_iota(jnp.int32, (LANES, 1), 0) < SSD_HEADDIM

    def col(x, h):
        return jnp.broadcast_to(x[:, h:h + 1], (rows, LANES))

    def pair(x, ha, hb):
        return jnp.where(low, col(x, ha), col(x, hb))

    pairs_per_group = SSD_GROUP_WIDTH // LANES
    for g in range(SSD_GROUPS):
        gs = slice(g * SSD_DSTATE, (g + 1) * SSD_DSTATE)
        b_g = b_all[:, gs]
        c_g = c_all[:, gs]
        cb = _nt(c_g, b_g)
        srows = slice(g * SSD_GROUP_WIDTH, (g + 1) * SSD_GROUP_WIDTH)
        y_off = _nt(c_g, st_ref[srows, :].astype(BF16))
        ssq = jnp.zeros((rows, 1), F32)
        for jj in range(pairs_per_group):
            j = g * pairs_per_group + jj
            ha, hb = 2 * j, 2 * j + 1
            cols = slice(j * LANES, (j + 1) * LANES)
            xs = xs_ref[:, cols]
            xdt = xs * pair(dt, ha, hb)
            y = jnp.zeros((rows, LANES), F32)
            for h, keep in ((ha, low), (hb, ~low)):
                seg = col(a_cs, h) - a_cs_t[h:h + 1, :]
                m_h = (cb * jnp.exp(jnp.where(causal, seg, -jnp.inf))).astype(BF16)
                y = y + jnp.dot(m_h, jnp.where(keep, xdt, 0.0).astype(BF16), preferred_element_type=F32)
            y = y + y_off[:, jj * LANES:(jj + 1) * LANES] * pair(e_cs, ha, hb)
            y = y + dskip_ref[:, cols] * xs
            x_end = (xdt * pair(d_end, ha, hb)).astype(BF16)
            prow = slice(j * LANES, (j + 1) * LANES)
            decay = jnp.where(top, c_decay[:, ha:ha + 1], c_decay[:, hb:hb + 1])
            st_ref[prow, :] = st_ref[prow, :] * decay + _tn(x_end, b_g)
            y = y * _silu(z_ref[0, :, cols].astype(F32))
            ssq = ssq + jnp.sum(y * y, axis=-1, keepdims=True)
            y_ref[:, cols] = y
        scale = lax.rsqrt(ssq * (1.0 / SSD_GROUP_WIDTH) + EPS)
        gcols = slice(g * SSD_GROUP_WIDTH, (g + 1) * SSD_GROUP_WIDTH)
        o_ref[0, :, gcols] = (y_ref[:, gcols] * scale * nw_ref[:, gcols]).astype(o_ref.dtype)


def _ssd(proj, small, conv_w, conv_b, dt_bias, a_log, d_exp, norm_w):
    bsz, L, _ = proj.shape
    row = lambda seg: (lambda b, n: (b, n, _seg_block(seg)))
    const = lambda b, n: (0, 0)
    return pl.pallas_call(
        _ssd_kernel,
        grid=(bsz, L // SSD_CHUNK),
        in_specs=[pl.BlockSpec((1, SSD_CHUNK, SSD_DINNER), row(SEG_Z)),
                  pl.BlockSpec((1, SSD_CHUNK, SSD_XBC), row(SEG_XBC)),
                  pl.BlockSpec((1, SSD_CHUNK, N_SMALL), lambda b, n: (b, n, 0)),
                  pl.BlockSpec((SSD_CONV, SSD_XBC), const),
                  pl.BlockSpec((1, SSD_XBC), const),
                  pl.BlockSpec((1, N_SMALL), const),
                  pl.BlockSpec((1, N_SMALL), const),
                  pl.BlockSpec((1, SSD_DINNER), const),
                  pl.BlockSpec((1, SSD_DINNER), const)],
        out_specs=pl.BlockSpec((1, SSD_CHUNK, SSD_DINNER), lambda b, n: (b, n, 0)),
        out_shape=jax.ShapeDtypeStruct((bsz, L, SSD_DINNER), BF16),
        scratch_shapes=[pltpu.VMEM((SSD_CHUNK + 2 * SUBLANES, SSD_XBC), F32),
                        pltpu.VMEM((SSD_CHUNK, SSD_DINNER), F32),
                        pltpu.VMEM((SSD_CHUNK, SSD_DINNER), F32),
                        pltpu.VMEM((SSD_DINNER, SSD_DSTATE), F32)],
        compiler_params=pltpu.CompilerParams(
            dimension_semantics=("arbitrary", "arbitrary"), vmem_limit_bytes=VMEM_LIMIT),
        name="ssd",
    )(proj, proj, small, conv_w, conv_b, dt_bias, a_log, d_exp, norm_w)


def _valid_rows(j, tm):
    pos = j * tm + lax.broadcasted_iota(jnp.int32, (tm, 1), 0)
    return (pos >= PAD).astype(F32)


def _merge_kernel(ga_ref, sw_ref, sd_ref, gl_ref, x_ref, wa_ref, wb_ref, wc_ref, wo_ref, bg_ref, o_ref):
    gates = jax.nn.sigmoid(gl_ref[0].astype(F32) + bg_ref[...])
    merged = gates[:, 0:D_MODEL] * jnp.dot(ga_ref[0], wa_ref[...], preferred_element_type=F32)
    merged += gates[:, D_MODEL:2 * D_MODEL] * jnp.dot(sw_ref[0], wb_ref[...], preferred_element_type=F32)
    merged += gates[:, 2 * D_MODEL:] * jnp.dot(sd_ref[0], wc_ref[...], preferred_element_type=F32)
    out = jnp.dot(merged.astype(BF16), wo_ref[...], preferred_element_type=F32)
    o_ref[0] = x_ref[0] + out * _valid_rows(pl.program_id(1), ROW_TILE)


def _merge(y_gla, y_swa, y_ssd, proj, x, wa, wb, wc, wo, b_gate):
    bsz, L, d = x.shape
    row = lambda b, j: (b, j, 0)
    const = lambda b, j: (0, 0)
    single = pl.Buffered(1)
    return pl.pallas_call(
        _merge_kernel,
        grid=(bsz, L // ROW_TILE),
        in_specs=[pl.BlockSpec((1, ROW_TILE, GLA_DV), row),
                  pl.BlockSpec((1, ROW_TILE, SWA_HQ * SWA_HD), row),
                  pl.BlockSpec((1, ROW_TILE, SSD_DINNER), row),
                  pl.BlockSpec((1, ROW_TILE, N_BRANCH * D_MODEL), lambda b, j: (b, j, _seg_block(SEG_GATE))),
                  pl.BlockSpec((1, ROW_TILE, d), row),
                  pl.BlockSpec((GLA_DV, d), const, pipeline_mode=single),
                  pl.BlockSpec((SWA_HQ * SWA_HD, d), const, pipeline_mode=single),
                  pl.BlockSpec((SSD_DINNER, d), const, pipeline_mode=single),
                  pl.BlockSpec((d, d), const, pipeline_mode=single),
                  pl.BlockSpec((1, N_BRANCH * D_MODEL), const)],
        out_specs=pl.BlockSpec((1, ROW_TILE, d), row),
        out_shape=jax.ShapeDtypeStruct((bsz, L, d), F32),
        compiler_params=pltpu.CompilerParams(
            dimension_semantics=("arbitrary", "arbitrary"), vmem_limit_bytes=VMEM_LIMIT),
        name="merge",
    )(y_gla, y_swa, y_ssd, proj, x, wa, wb, wc, wo, b_gate)


def _mlp_kernel(x_ref, nm_ref, w1_ref, w2_ref, gn_ref, xo_ref, hn_ref):
    x = x_ref[0]
    h2 = _rms(x, nm_ref[...]).astype(BF16)
    acc = jnp.zeros((ROW_TILE, D_MODEL), F32)
    ff_chunk = 1024
    for c in range(D_FF // ff_chunk):
        cols = slice(c * ff_chunk, (c + 1) * ff_chunk)
        hid = jnp.dot(h2, w1_ref[:, cols], preferred_element_type=F32)
        hid = jnp.square(jnp.maximum(hid, 0.0)).astype(BF16)
        acc = acc + jnp.dot(hid, w2_ref[cols, :], preferred_element_type=F32)
    xo = x + acc * _valid_rows(pl.program_id(1), ROW_TILE)
    xo_ref[0] = xo
    hn_ref[0] = _rms(xo, gn_ref[...]).astype(hn_ref.dtype)


def _mlp(x, norm_w, w1, w2, next_norm_w, hn_dtype):
    bsz, L, d = x.shape
    row = lambda b, j: (b, j, 0)
    const = lambda b, j: (0, 0)
    single = pl.Buffered(1)
    return pl.pallas_call(
        _mlp_kernel,
        grid=(bsz, L // ROW_TILE),
        in_specs=[pl.BlockSpec((1, ROW_TILE, d), row),
                  pl.BlockSpec((1, d), const),
                  pl.BlockSpec((d, D_FF), const, pipeline_mode=single),
                  pl.BlockSpec((D_FF, d), const, pipeline_mode=single),
                  pl.BlockSpec((1, d), const)],
        out_specs=[pl.BlockSpec((1, ROW_TILE, d), row),
                   pl.BlockSpec((1, ROW_TILE, d), row)],
        out_shape=[jax.ShapeDtypeStruct((bsz, L, d), F32),
                   jax.ShapeDtypeStruct((bsz, L, d), hn_dtype)],
        compiler_params=pltpu.CompilerParams(
            dimension_semantics=("arbitrary", "arbitrary"), vmem_limit_bytes=VMEM_LIMIT),
        name="mlp",
    )(x, norm_w, w1, w2, next_norm_w)


def _split_w_in(w_in):
    sizes = (GLA_DK, GLA_DK, GLA_DV, GLA_DV, GLA_RANK, SWA_HQ * SWA_HD, SWA_HKV * SWA_HD,
             SWA_HKV * SWA_HD, SSD_DINNER, SSD_XBC, SSD_HEADS, N_BRANCH * D_MODEL)
    starts = [sum(sizes[:i]) for i in range(len(sizes))]
    part = lambda i: w_in[:, :, starts[i]:starts[i] + sizes[i]]
    g_q, g_k, g_v, g_r, g_lr, s_q, s_k, s_v, m_z, m_xbc, m_dt, gate = (part(i) for i in range(len(sizes)))
    big = jnp.concatenate([gate, m_xbc, m_z, g_v, g_r, s_q, g_q, g_k, s_k, s_v], axis=-1).astype(BF16)
    pad = jnp.zeros(w_in.shape[:2] + (N_SMALL - SSD_HEADS - GLA_RANK,), w_in.dtype)
    small = jnp.concatenate([m_dt, g_lr, pad], axis=-1).astype(BF16)
    return big, small


def _pad_lanes(v, offset=0):
    out = jnp.zeros((v.shape[0], 1, LANES), F32)
    return out.at[:, 0, offset:offset + v.shape[1]].set(v.astype(F32))


def kernel(x, meta_tokens, norm_mix, w_in, b_gate, gla_w_gate_up, gla_b_gate_up, gla_norm, swa_sinks,
           ssd_conv_w, ssd_conv_b, ssd_dt_bias, ssd_A_log, ssd_D, ssd_norm,
           w_branch_gla, w_branch_swa, w_branch_ssd, w_out, norm_mlp, w_mlp_in, w_mlp_out, final_norm):
    bsz, seq, d = x.shape
    L = LEAD + seq
    h = jnp.concatenate([jnp.zeros((bsz, PAD, d), x.dtype),
                         jnp.broadcast_to(meta_tokens.astype(x.dtype)[None], (bsz, N_META, d)),
                         x], axis=1)

    w_big, w_small = _split_w_in(w_in)
    wg = jnp.zeros((DEPTH, N_SMALL, GLA_DK), F32).at[:, SMALL_LR:SMALL_LR + GLA_RANK, :].set(gla_w_gate_up)
    wg = wg.astype(BF16)
    dt_bias = _pad_lanes(ssd_dt_bias, SMALL_DT)
    a_log = _pad_lanes(ssd_A_log, SMALL_DT)
    d_exp = jnp.repeat(ssd_D.astype(F32), SSD_HEADDIM, axis=-1)[:, None, :]
    wa, wb, wc, wo = (w.astype(BF16) for w in (w_branch_gla, w_branch_swa, w_branch_ssd, w_out))
    w1, w2 = w_mlp_in.astype(BF16), w_mlp_out.astype(BF16)
    row = lambda p: p.astype(F32)[:, None, :]
    norm_mix_r, norm_mlp_r, b_gate_r = row(norm_mix), row(norm_mlp), row(b_gate)
    gla_bg_r, gla_nw_r = row(gla_b_gate_up), row(gla_norm)
    conv_b_r, ssd_nw_r = row(ssd_conv_b), row(ssd_norm)
    final_r = final_norm.astype(F32)[None, :]

    hn = _norm(h, norm_mix_r[0])
    for i in range(DEPTH):
        proj, small = _inproj(hn.reshape(bsz * L, d), w_big[i], w_small[i])
        proj = proj.reshape(bsz, L, N_BIG)
        small = small.reshape(bsz, L, N_SMALL)
        y_gla = _gla(proj, small, wg[i], gla_bg_r[i], gla_nw_r[i])
        y_swa = _swa(proj, swa_sinks[i].astype(F32))
        y_ssd = _ssd(proj, small, ssd_conv_w[i].astype(F32), conv_b_r[i], dt_bias[i], a_log[i],
                     d_exp[i], ssd_nw_r[i])
        h = _merge(y_gla, y_swa, y_ssd, proj, h, wa[i], wb[i], wc[i], wo[i], b_gate_r[i])
        last = i == DEPTH - 1
        next_g = final_r if last else norm_mix_r[i + 1]
        h, hn = _mlp(h, norm_mlp_r[i], w1[i], w2[i], next_g, F32 if last else BF16)
    return hn[:, LEAD:, :]
```

```python
import functools

import jax
import jax.numpy as jnp
from jax import lax
from jax.experimental import pallas as pl
from jax.experimental.pallas import tpu as pltpu

F32 = jnp.float32
BF16 = jnp.bfloat16

D_MODEL = 1024
DEPTH = 4
N_META = 16
LEAD = 128
PAD = LEAD - N_META
EPS = 1e-6

GLA_HEADS = 4
GLA_DK = D_MODEL // 2
GLA_DV = D_MODEL
GLA_HK = GLA_DK // GLA_HEADS
GLA_HV = GLA_DV // GLA_HEADS
GLA_RANK = 16
GLA_NORMALIZER = 16.0
GLA_CHUNK = 64

SWA_HQ = 16
SWA_HKV = 2
SWA_HD = 64
SWA_BLOCK = 128

SSD_DINNER = 2 * D_MODEL
SSD_HEADDIM = 64
SSD_HEADS = SSD_DINNER // SSD_HEADDIM
SSD_GROUPS = 4
SSD_DSTATE = 128
SSD_CONV = 4
SSD_CHUNK = 128
SSD_BC = SSD_GROUPS * SSD_DSTATE
SSD_XBC = SSD_DINNER + 2 * SSD_BC
SSD_GROUP_WIDTH = SSD_DINNER // SSD_GROUPS

N_BRANCH = 3
D_FF = 4 * D_MODEL

LANES = 128
SUBLANES = 8

SEG_GATE, SEG_XBC, SEG_Z, SEG_GV, SEG_GR, SEG_SQ, SEG_GQ, SEG_GK, SEG_SK, SEG_SV = range(10)
_SEG_WIDTHS = (N_BRANCH * D_MODEL, SSD_XBC, SSD_DINNER, GLA_DV, GLA_DV,
               SWA_HQ * SWA_HD, GLA_DK, GLA_DK, SWA_HKV * SWA_HD, SWA_HKV * SWA_HD)
_SEG_STARTS = tuple(sum(_SEG_WIDTHS[:i]) for i in range(len(_SEG_WIDTHS)))
N_BIG = sum(_SEG_WIDTHS)
assert all(s % w == 0 for s, w in zip(_SEG_STARTS, _SEG_WIDTHS))
SMALL_DT = 0
SMALL_LR = SSD_HEADS
N_SMALL = LANES

PROJ_TM = 1024
PROJ_TN = 1792
ROW_TILE = 384
GLA_ROWS = 128
VMEM_LIMIT = 56 * 1024 * 1024


def _seg_block(seg):
    return _SEG_STARTS[seg] // _SEG_WIDTHS[seg]


def _nt(a, b):
    return lax.dot_general(a, b, (((1,), (1,)), ((), ())), preferred_element_type=F32)


def _tn(a, b):
    return lax.dot_general(a, b, (((0,), (0,)), ((), ())), preferred_element_type=F32)


def _split_dot(m01, x):
    hi = x.astype(BF16)
    r1 = x - hi.astype(F32)
    mid = r1.astype(BF16)
    lo = (r1 - mid.astype(F32)).astype(BF16)
    return (jnp.dot(m01, hi, preferred_element_type=F32)
            + jnp.dot(m01, mid, preferred_element_type=F32)
            + jnp.dot(m01, lo, preferred_element_type=F32))


def _tril_ones(n):
    r = lax.broadcasted_iota(jnp.int32, (n, n), 0)
    c = lax.broadcasted_iota(jnp.int32, (n, n), 1)
    return r >= c


def _log1p_exp_neg_abs(x):
    u = jnp.exp(-jnp.abs(x))
    w = 1.0 + u
    d = w - 1.0
    return jnp.where(d == 0.0, u, jnp.log(w) * (u / d))


def _softplus(x):
    return jnp.maximum(x, 0.0) + _log1p_exp_neg_abs(x)


def _log_sigmoid(x):
    return jnp.minimum(x, 0.0) - _log1p_exp_neg_abs(x)


def _silu(x):
    return x * jax.nn.sigmoid(x)


def _rms(x, g):
    return x * lax.rsqrt(jnp.mean(x * x, axis=-1, keepdims=True) + EPS) * g


def _norm_kernel(x_ref, g_ref, o_ref):
    o_ref[0] = _rms(x_ref[0], g_ref[...]).astype(o_ref.dtype)


def _norm(x, g):
    bsz, L, d = x.shape
    return pl.pallas_call(
        _norm_kernel,
        grid=(bsz, L // ROW_TILE),
        in_specs=[pl.BlockSpec((1, ROW_TILE, d), lambda b, j: (b, j, 0)),
                  pl.BlockSpec((1, d), lambda b, j: (0, 0))],
        out_specs=pl.BlockSpec((1, ROW_TILE, d), lambda b, j: (b, j, 0)),
        out_shape=jax.ShapeDtypeStruct((bsz, L, d), BF16),
        name="first_norm",
    )(x, g)


def _inproj_kernel(h_ref, w_ref, ws_ref, o_ref, os_ref):
    o_ref[...] = jnp.dot(h_ref[...], w_ref[...], preferred_element_type=F32).astype(o_ref.dtype)

    @pl.when(pl.program_id(1) == 0)
    def _():
        os_ref[...] = jnp.dot(h_ref[...], ws_ref[...], preferred_element_type=F32)


def _inproj(hn, w_big, w_small):
    T, d = hn.shape
    return pl.pallas_call(
        _inproj_kernel,
        grid=(T // PROJ_TM, N_BIG // PROJ_TN),
        in_specs=[pl.BlockSpec((PROJ_TM, d), lambda i, j: (i, 0)),
                  pl.BlockSpec((d, PROJ_TN), lambda i, j: (0, j)),
                  pl.BlockSpec((d, N_SMALL), lambda i, j: (0, 0))],
        out_specs=[pl.BlockSpec((PROJ_TM, PROJ_TN), lambda i, j: (i, j)),
                   pl.BlockSpec((PROJ_TM, N_SMALL), lambda i, j: (i, 0))],
        out_shape=[jax.ShapeDtypeStruct((T, N_BIG), BF16),
                   jax.ShapeDtypeStruct((T, N_SMALL), F32)],
        compiler_params=pltpu.CompilerParams(
            dimension_semantics=("arbitrary", "arbitrary"), vmem_limit_bytes=VMEM_LIMIT),
        name="in_proj",
    )(hn, w_big, w_small)


def _gla_kernel(q_ref, k_ref, v_ref, r_ref, sm_ref, wg_ref, bg_ref, nw_ref, o_ref, st_ref):
    @pl.when(pl.program_id(1) == 0)
    def _():
        st_ref[...] = jnp.zeros_like(st_ref)

    logits = jnp.dot(sm_ref[0].astype(BF16), wg_ref[...], preferred_element_type=F32) + bg_ref[...]
    gk_all = _log_sigmoid(logits) * (1.0 / GLA_NORMALIZER)
    causal = _tril_ones(GLA_CHUNK)
    tril = causal.astype(BF16)
    for c in range(GLA_ROWS // GLA_CHUNK):
        rows = slice(c * GLA_CHUNK, (c + 1) * GLA_CHUNK)
        g_all = _split_dot(tril, gk_all[rows])
        for h in range(GLA_HEADS):
            hk = slice(h * GLA_HK, (h + 1) * GLA_HK)
            hv = slice(h * GLA_HV, (h + 1) * GLA_HV)
            g = g_all[:, hk]
            g_last = g[GLA_CHUNK - 1:GLA_CHUNK, :]
            q = q_ref[0, rows, hk].astype(F32) * (GLA_HK ** -0.5)
            k = k_ref[0, rows, hk].astype(F32)
            v = v_ref[0, rows, hv]
            q_dec = (q * jnp.exp(g)).astype(BF16)
            k_inv = (k * jnp.exp(-g)).astype(BF16)
            k_end = (k * jnp.exp(g_last - g)).astype(BF16)
            attn = jnp.where(causal, _nt(q_dec, k_inv), 0.0).astype(BF16)
            state_t = st_ref[h]
            o = jnp.dot(attn, v, preferred_element_type=F32) + _nt(q_dec, state_t.astype(BF16))
            st_ref[h] = state_t * jnp.exp(g_last) + _tn(v, k_end)
            o = o * lax.rsqrt(jnp.mean(o * o, axis=-1, keepdims=True) + EPS) * nw_ref[...]
            o = o * _silu(r_ref[0, rows, hv].astype(F32))
            o_ref[0, rows, hv] = o.astype(o_ref.dtype)


def _gla(proj, small, wg, bg, nw):
    bsz, L, _ = proj.shape
    row = lambda seg: (lambda b, n: (b, n, _seg_block(seg)))
    const = lambda b, n: (0, 0)
    return pl.pallas_call(
        _gla_kernel,
        grid=(bsz, L // GLA_ROWS),
        in_specs=[pl.BlockSpec((1, GLA_ROWS, GLA_DK), row(SEG_GQ)),
                  pl.BlockSpec((1, GLA_ROWS, GLA_DK), row(SEG_GK)),
                  pl.BlockSpec((1, GLA_ROWS, GLA_DV), row(SEG_GV)),
                  pl.BlockSpec((1, GLA_ROWS, GLA_DV), row(SEG_GR)),
                  pl.BlockSpec((1, GLA_ROWS, N_SMALL), lambda b, n: (b, n, 0)),
                  pl.BlockSpec((N_SMALL, GLA_DK), const),
                  pl.BlockSpec((1, GLA_DK), const),
                  pl.BlockSpec((1, GLA_HV), const)],
        out_specs=pl.BlockSpec((1, GLA_ROWS, GLA_DV), lambda b, n: (b, n, 0)),
        out_shape=jax.ShapeDtypeStruct((bsz, L, GLA_DV), BF16),
        scratch_shapes=[pltpu.VMEM((GLA_HEADS, GLA_HV, GLA_HK), F32)],
        compiler_params=pltpu.CompilerParams(
            dimension_semantics=("arbitrary", "arbitrary"), vmem_limit_bytes=VMEM_LIMIT),
        name="gla",
    )(proj, proj, proj, proj, small, wg, bg, nw)


def _swa_kernel(q_ref, kp_ref, kc_ref, vp_ref, vc_ref, sink_ref, o_ref):
    n = pl.program_id(1)
    nk = 2 * SWA_BLOCK
    low_q = lax.broadcasted_iota(jnp.int32, (SWA_BLOCK, LANES), 1) < SWA_HD
    low_k = lax.broadcasted_iota(jnp.int32, (nk, LANES), 1) < SWA_HD

    def per_kv_head(prev_ref, cur_ref):
        both = jnp.concatenate([prev_ref[0], cur_ref[0]], axis=0).astype(F32)
        swapped = pltpu.roll(both, SWA_HD, axis=1)
        return (jnp.where(low_k, both, swapped).astype(BF16),
                jnp.where(low_k, swapped, both).astype(BF16))

    k_ext = per_kv_head(kp_ref, kc_ref)
    both_v = jnp.concatenate([vp_ref[0], vc_ref[0]], axis=0).astype(F32)
    swapped_v = pltpu.roll(both_v, SWA_HD, axis=1)
    v_ones = []
    for in_low, in_high in ((both_v, swapped_v), (swapped_v, both_v)):
        v_ones.append((jnp.where(low_k, in_low, 1.0).astype(BF16),
                       jnp.where(low_k, 1.0, in_high).astype(BF16)))
    qi = lax.broadcasted_iota(jnp.int32, (SWA_BLOCK, nk), 0)
    kk = lax.broadcasted_iota(jnp.int32, (SWA_BLOCK, nk), 1)
    allowed = (kk > qi) & (kk <= qi + SWA_BLOCK) & (kk >= PAD + SWA_BLOCK - SWA_BLOCK * n)
    heads_per_kv = SWA_HQ // SWA_HKV
    for j in range(SWA_HQ // 2):
        hkv = (2 * j) // heads_per_kv
        cols = slice(j * LANES, (j + 1) * LANES)
        q_pair = q_ref[0, :, cols] * (SWA_HD ** -0.5)
        full, sink_term = [], []
        for half in range(2):
            q_half = jnp.where(low_q if half == 0 else ~low_q, q_pair, jnp.zeros_like(q_pair))
            s = jnp.where(allowed, _nt(q_half, k_ext[hkv]), -jnp.inf)
            sink = sink_ref[2 * j + half]
            m = jnp.maximum(jnp.max(s, axis=-1, keepdims=True), sink)
            p = jnp.exp(s - m).astype(BF16)
            full.append(jnp.dot(p, v_ones[hkv][half], preferred_element_type=F32))
            sink_term.append(jnp.exp(sink - m))
        num = jnp.where(low_q, full[0], full[1])
        sums = pltpu.roll(jnp.where(low_q, full[1], full[0]), SWA_HD, axis=1)
        den = sums + jnp.where(low_q, sink_term[0], sink_term[1])
        o_ref[0, :, cols] = (num * (1.0 / den)).astype(o_ref.dtype)


def _swa(proj, sinks):
    bsz, L, _ = proj.shape
    kvw = SWA_HKV * SWA_HD
    cur = lambda seg: (lambda b, n: (b, n, _seg_block(seg)))
    prev = lambda seg: (lambda b, n: (b, jnp.maximum(n - 1, 0), _seg_block(seg)))
    return pl.pallas_call(
        _swa_kernel,
        grid=(bsz, L // SWA_BLOCK),
        in_specs=[pl.BlockSpec((1, SWA_BLOCK, SWA_HQ * SWA_HD), cur(SEG_SQ)),
                  pl.BlockSpec((1, SWA_BLOCK, kvw), prev(SEG_SK)),
                  pl.BlockSpec((1, SWA_BLOCK, kvw), cur(SEG_SK)),
                  pl.BlockSpec((1, SWA_BLOCK, kvw), prev(SEG_SV)),
                  pl.BlockSpec((1, SWA_BLOCK, kvw), cur(SEG_SV)),
                  pl.BlockSpec(memory_space=pltpu.SMEM)],
        out_specs=pl.BlockSpec((1, SWA_BLOCK, SWA_HQ * SWA_HD), lambda b, n: (b, n, 0)),
        out_shape=jax.ShapeDtypeStruct((bsz, L, SWA_HQ * SWA_HD), BF16),
        compiler_params=pltpu.CompilerParams(
            dimension_semantics=("arbitrary", "arbitrary"), vmem_limit_bytes=VMEM_LIMIT),
        name="swa",
    )(proj, proj, proj, proj, proj, sinks)


def _ssd_kernel(z_ref, xbc_ref, sm_ref, cw_ref, cb_ref, dtb_ref, alog_ref, dskip_ref, nw_ref, shift_ref,
                o_ref, ext_ref, xs_ref, y_ref, st_ref):
    n = pl.program_id(1)
    rows = SSD_CHUNK

    @pl.when(n == 0)
    def _():
        ext_ref[0:rows, :] = jnp.zeros((rows, SSD_XBC), BF16)
        st_ref[...] = jnp.zeros_like(st_ref)

    ext_ref[rows:2 * rows, :] = xbc_ref[0]
    col_chunk = 512
    bc_parts = []
    for c in range(SSD_XBC // col_chunk):
        cols = slice(c * col_chunk, (c + 1) * col_chunk)
        shifted = jnp.dot(shift_ref[...], ext_ref[:, cols], preferred_element_type=F32)
        acc = cb_ref[:, cols] + cw_ref[SSD_CONV - 1:SSD_CONV, cols] * xbc_ref[0, :, cols].astype(F32)
        for back in range(1, SSD_CONV):
            tap = SSD_CONV - 1 - back
            acc = acc + cw_ref[tap:tap + 1, cols] * shifted[(back - 1) * rows:back * rows, :]
        act = _silu(acc)
        if c < SSD_DINNER // col_chunk:
            xs_ref[:, cols] = act
        else:
            bc_parts.append(act.astype(BF16))
    ext_ref[0:rows, :] = xbc_ref[0]
    b_all, c_all = bc_parts

    lane = lax.broadcasted_iota(jnp.int32, (1, LANES), 1)
    pos = n * rows + lax.broadcasted_iota(jnp.int32, (rows, 1), 0)
    valid = (pos >= PAD).astype(F32)
    dt = _softplus(sm_ref[0] + dtb_ref[...]) * valid
    a_coef = jnp.where(lane < SSD_HEADS, -jnp.exp(alog_ref[...]), 0.0)
    causal = _tril_ones(rows)
    a_cs = _split_dot(causal.astype(BF16), dt * a_coef)
    a_cs_t = a_cs.T
    a_last_t = a_cs_t[:, rows - 1:rows]
    dt_t = dt.T
    w_end_t = dt_t * jnp.exp(a_last_t - a_cs_t)
    c_decay_t = jnp.exp(a_last_t)

    low = lax.broadcasted_iota(jnp.int32, (rows, LANES), 1) < SSD_HEADDIM
    top = lax.broadcasted_iota(jnp.int32, (LANES, 1), 0) < SSD_HEADDIM

    def col(x, h):
        return jnp.broadcast_to(x[:, h:h + 1], (rows, LANES))

    pairs_per_group = SSD_GROUP_WIDTH // LANES
    for g in range(SSD_GROUPS):
        gs = slice(g * SSD_DSTATE, (g + 1) * SSD_DSTATE)
        b_g = b_all[:, gs]
        c_g = c_all[:, gs]
        cb = _nt(c_g, b_g)
        srows = slice(g * SSD_GROUP_WIDTH, (g + 1) * SSD_GROUP_WIDTH)
        y_off = _nt(c_g, st_ref[srows, :].astype(BF16))
        ssq = jnp.zeros((rows, 1), F32)
        for jj in range(pairs_per_group):
            j = g * pairs_per_group + jj
            ha, hb = 2 * j, 2 * j + 1
            cols = slice(j * LANES, (j + 1) * LANES)
            xs = xs_ref[:, cols]
            xs_bf = xs.astype(BF16)
            a_col = (col(a_cs, ha), col(a_cs, hb))
            y = jnp.zeros((rows, LANES), F32)
            for h, a_c, keep in ((ha, a_col[0], low), (hb, a_col[1], ~low)):
                seg = a_c - a_cs_t[h:h + 1, :]
                m_h = cb * jnp.exp(jnp.where(causal, seg, -jnp.inf)) * dt_t[h:h + 1, :]
                y = y + jnp.dot(m_h.astype(BF16), jnp.where(keep, xs_bf, jnp.zeros_like(xs_bf)),
                                preferred_element_type=F32)
            y = y + y_off[:, jj * LANES:(jj + 1) * LANES] * jnp.exp(jnp.where(low, a_col[0], a_col[1]))
            y = y + dskip_ref[:, cols] * xs
            w_rows = jnp.where(top, w_end_t[ha:ha + 1, :], w_end_t[hb:hb + 1, :])
            x_end_t = (xs.T * w_rows).astype(BF16)
            prow = slice(j * LANES, (j + 1) * LANES)
            decay = jnp.where(top, c_decay_t[ha:ha + 1, :], c_decay_t[hb:hb + 1, :])
            st_ref[prow, :] = st_ref[prow, :] * decay + jnp.dot(x_end_t, b_g, preferred_element_type=F32)
            y = y * _silu(z_ref[0, :, cols].astype(F32))
            ssq = ssq + jnp.sum(y * y, axis=-1, keepdims=True)
            y_ref[:, cols] = y
        scale = lax.rsqrt(ssq * (1.0 / SSD_GROUP_WIDTH) + EPS)
        gcols = slice(g * SSD_GROUP_WIDTH, (g + 1) * SSD_GROUP_WIDTH)
        o_ref[0, :, gcols] = (y_ref[:, gcols] * scale * nw_ref[:, gcols]).astype(o_ref.dtype)


def _conv_shift_matrix():
    r = jnp.arange((SSD_CONV - 1) * SSD_CHUNK)
    source = SSD_CHUNK + r % SSD_CHUNK - (r // SSD_CHUNK + 1)
    return (jnp.arange(2 * SSD_CHUNK)[None, :] == source[:, None]).astype(BF16)


def _ssd(proj, small, conv_w, conv_b, dt_bias, a_log, d_exp, norm_w):
    bsz, L, _ = proj.shape
    row = lambda seg: (lambda b, n: (b, n, _seg_block(seg)))
    const = lambda b, n: (0, 0)
    return pl.pallas_call(
        _ssd_kernel,
        grid=(bsz, L // SSD_CHUNK),
        in_specs=[pl.BlockSpec((1, SSD_CHUNK, SSD_DINNER), row(SEG_Z)),
                  pl.BlockSpec((1, SSD_CHUNK, SSD_XBC), row(SEG_XBC)),
                  pl.BlockSpec((1, SSD_CHUNK, N_SMALL), lambda b, n: (b, n, 0)),
                  pl.BlockSpec((SSD_CONV, SSD_XBC), const),
                  pl.BlockSpec((1, SSD_XBC), const),
                  pl.BlockSpec((1, N_SMALL), const),
                  pl.BlockSpec((1, N_SMALL), const),
                  pl.BlockSpec((1, SSD_DINNER), const),
                  pl.BlockSpec((1, SSD_DINNER), const),
                  pl.BlockSpec(((SSD_CONV - 1) * SSD_CHUNK, 2 * SSD_CHUNK), const)],
        out_specs=pl.BlockSpec((1, SSD_CHUNK, SSD_DINNER), lambda b, n: (b, n, 0)),
        out_shape=jax.ShapeDtypeStruct((bsz, L, SSD_DINNER), BF16),
        scratch_shapes=[pltpu.VMEM((2 * SSD_CHUNK, SSD_XBC), BF16),
                        pltpu.VMEM((SSD_CHUNK, SSD_DINNER), F32),
                        pltpu.VMEM((SSD_CHUNK, SSD_DINNER), F32),
                        pltpu.VMEM((SSD_DINNER, SSD_DSTATE), F32)],
        compiler_params=pltpu.CompilerParams(
            dimension_semantics=("arbitrary", "arbitrary"), vmem_limit_bytes=VMEM_LIMIT),
        name="ssd",
    )(proj, proj, small, conv_w, conv_b, dt_bias, a_log, d_exp, norm_w, _conv_shift_matrix())


def _valid_rows(j, tm):
    pos = j * tm + lax.broadcasted_iota(jnp.int32, (tm, 1), 0)
    return (pos >= PAD).astype(F32)


def _merge_kernel(ga_ref, sw_ref, sd_ref, gl_ref, x_ref, wa_ref, wb_ref, wc_ref, wo_ref, bg_ref, o_ref):
    gates = jax.nn.sigmoid(gl_ref[0].astype(F32) + bg_ref[...])
    merged = gates[:, 0:D_MODEL] * jnp.dot(ga_ref[0], wa_ref[...], preferred_element_type=F32)
    merged += gates[:, D_MODEL:2 * D_MODEL] * jnp.dot(sw_ref[0], wb_ref[...], preferred_element_type=F32)
    merged += gates[:, 2 * D_MODEL:] * jnp.dot(sd_ref[0], wc_ref[...], preferred_element_type=F32)
    out = jnp.dot(merged.astype(BF16), wo_ref[...], preferred_element_type=F32)
    o_ref[0] = x_ref[0] + out * _valid_rows(pl.program_id(1), ROW_TILE)


def _merge(y_gla, y_swa, y_ssd, proj, x, wa, wb, wc, wo, b_gate):
    bsz, L, d = x.shape
    row = lambda b, j: (b, j, 0)
    const = lambda b, j: (0, 0)
    single = pl.Buffered(1)
    return pl.pallas_call(
        _merge_kernel,
        grid=(bsz, L // ROW_TILE),
        in_specs=[pl.BlockSpec((1, ROW_TILE, GLA_DV), row),
                  pl.BlockSpec((1, ROW_TILE, SWA_HQ * SWA_HD), row),
                  pl.BlockSpec((1, ROW_TILE, SSD_DINNER), row),
                  pl.BlockSpec((1, ROW_TILE, N_BRANCH * D_MODEL), lambda b, j: (b, j, _seg_block(SEG_GATE))),
                  pl.BlockSpec((1, ROW_TILE, d), row),
                  pl.BlockSpec((GLA_DV, d), const, pipeline_mode=single),
                  pl.BlockSpec((SWA_HQ * SWA_HD, d), const, pipeline_mode=single),
                  pl.BlockSpec((SSD_DINNER, d), const, pipeline_mode=single),
                  pl.BlockSpec((d, d), const, pipeline_mode=single),
                  pl.BlockSpec((1, N_BRANCH * D_MODEL), const)],
        out_specs=pl.BlockSpec((1, ROW_TILE, d), row),
        out_shape=jax.ShapeDtypeStruct((bsz, L, d), F32),
        compiler_params=pltpu.CompilerParams(
            dimension_semantics=("arbitrary", "arbitrary"), vmem_limit_bytes=VMEM_LIMIT),
        name="merge",
    )(y_gla, y_swa, y_ssd, proj, x, wa, wb, wc, wo, b_gate)


def _mlp_kernel(x_ref, nm_ref, w1_ref, w2_ref, gn_ref, xo_ref, hn_ref):
    x = x_ref[0]
    h2 = _rms(x, nm_ref[...]).astype(BF16)
    acc = jnp.zeros((ROW_TILE, D_MODEL), F32)
    ff_chunk = 1024
    for c in range(D_FF // ff_chunk):
        cols = slice(c * ff_chunk, (c + 1) * ff_chunk)
        hid = jnp.dot(h2, w1_ref[:, cols], preferred_element_type=F32)
        hid = jnp.square(jnp.maximum(hid, 0.0)).astype(BF16)
        acc = acc + jnp.dot(hid, w2_ref[cols, :], preferred_element_type=F32)
    xo = x + acc * _valid_rows(pl.program_id(1), ROW_TILE)
    xo_ref[0] = xo
    hn_ref[0] = _rms(xo, gn_ref[...]).astype(hn_ref.dtype)


def _mlp(x, norm_w, w1, w2, next_norm_w, hn_dtype):
    bsz, L, d = x.shape
    row = lambda b, j: (b, j, 0)
    const = lambda b, j: (0, 0)
    single = pl.Buffered(1)
    return pl.pallas_call(
        _mlp_kernel,
        grid=(bsz, L // ROW_TILE),
        in_specs=[pl.BlockSpec((1, ROW_TILE, d), row),
                  pl.BlockSpec((1, d), const),
                  pl.BlockSpec((d, D_FF), const, pipeline_mode=single),
                  pl.BlockSpec((D_FF, d), const, pipeline_mode=single),
                  pl.BlockSpec((1, d), const)],
        out_specs=[pl.BlockSpec((1, ROW_TILE, d), row),
                   pl.BlockSpec((1, ROW_TILE, d), row)],
        out_shape=[jax.ShapeDtypeStruct((bsz, L, d), F32),
                   jax.ShapeDtypeStruct((bsz, L, d), hn_dtype)],
        compiler_params=pltpu.CompilerParams(
            dimension_semantics=("arbitrary", "arbitrary"), vmem_limit_bytes=VMEM_LIMIT),
        name="mlp",
    )(x, norm_w, w1, w2, next_norm_w)


def _split_w_in(w_in):
    sizes = (GLA_DK, GLA_DK, GLA_DV, GLA_DV, GLA_RANK, SWA_HQ * SWA_HD, SWA_HKV * SWA_HD,
             SWA_HKV * SWA_HD, SSD_DINNER, SSD_XBC, SSD_HEADS, N_BRANCH * D_MODEL)
    starts = [sum(sizes[:i]) for i in range(len(sizes))]
    part = lambda i: w_in[:, :, starts[i]:starts[i] + sizes[i]]
    g_q, g_k, g_v, g_r, g_lr, s_q, s_k, s_v, m_z, m_xbc, m_dt, gate = (part(i) for i in range(len(sizes)))
    big = jnp.concatenate([gate, m_xbc, m_z, g_v, g_r, s_q, g_q, g_k, s_k, s_v], axis=-1).astype(BF16)
    pad = jnp.zeros(w_in.shape[:2] + (N_SMALL - SSD_HEADS - GLA_RANK,), w_in.dtype)
    small = jnp.concatenate([m_dt, g_lr, pad], axis=-1).astype(BF16)
    return big, small


def _pad_lanes(v, offset=0):
    out = jnp.zeros((v.shape[0], 1, LANES), F32)
    return out.at[:, 0, offset:offset + v.shape[1]].set(v.astype(F32))


def kernel(x, meta_tokens, norm_mix, w_in, b_gate, gla_w_gate_up, gla_b_gate_up, gla_norm, swa_sinks,
           ssd_conv_w, ssd_conv_b, ssd_dt_bias, ssd_A_log, ssd_D, ssd_norm,
           w_branch_gla, w_branch_swa, w_branch_ssd, w_out, norm_mlp, w_mlp_in, w_mlp_out, final_norm):
    bsz, seq, d = x.shape
    L = LEAD + seq
    h = jnp.concatenate([jnp.zeros((bsz, PAD, d), x.dtype),
                         jnp.broadcast_to(meta_tokens.astype(x.dtype)[None], (bsz, N_META, d)),
                         x], axis=1)

    w_big, w_small = _split_w_in(w_in)
    wg = jnp.zeros((DEPTH, N_SMALL, GLA_DK), F32).at[:, SMALL_LR:SMALL_LR + GLA_RANK, :].set(gla_w_gate_up)
    wg = wg.astype(BF16)
    dt_bias = _pad_lanes(ssd_dt_bias, SMALL_DT)
    a_log = _pad_lanes(ssd_A_log, SMALL_DT)
    d_exp = jnp.repeat(ssd_D.astype(F32), SSD_HEADDIM, axis=-1)[:, None, :]
    wa, wb, wc, wo = (w.astype(BF16) for w in (w_branch_gla, w_branch_swa, w_branch_ssd, w_out))
    w1, w2 = w_mlp_in.astype(BF16), w_mlp_out.astype(BF16)
    row = lambda p: p.astype(F32)[:, None, :]
    norm_mix_r, norm_mlp_r, b_gate_r = row(norm_mix), row(norm_mlp), row(b_gate)
    gla_bg_r, gla_nw_r = row(gla_b_gate_up), row(gla_norm)
    conv_b_r, ssd_nw_r = row(ssd_conv_b), row(ssd_norm)
    final_r = final_norm.astype(F32)[None, :]

    hn = _norm(h, norm_mix_r[0])
    for i in range(DEPTH):
        proj, small = _inproj(hn.reshape(bsz * L, d), w_big[i], w_small[i])
        proj = proj.reshape(bsz, L, N_BIG)
        small = small.reshape(bsz, L, N_SMALL)
        y_gla = _gla(proj, small, wg[i], gla_bg_r[i], gla_nw_r[i])
        y_swa = _swa(proj, swa_sinks[i].astype(F32))
        y_ssd = _ssd(proj, small, ssd_conv_w[i].astype(F32), conv_b_r[i], dt_bias[i], a_log[i],
                     d_exp[i], ssd_nw_r[i])
        h = _merge(y_gla, y_swa, y_ssd, proj, h, wa[i], wb[i], wc[i], wo[i], b_gate_r[i])
        last = i == DEPTH - 1
        next_g = final_r if last else norm_mix_r[i + 1]
        h, hn = _mlp(h, norm_mlp_r[i], w1[i], w2[i], next_g, F32 if last else BF16)
    return hn[:, LEAD:, :]
```

```python
import functools

import jax
import jax.numpy as jnp
from jax import lax
from jax.experimental import pallas as pl
from jax.experimental.pallas import tpu as pltpu

F32 = jnp.float32
BF16 = jnp.bfloat16

D_MODEL = 1024
DEPTH = 4
N_META = 16
LEAD = 128
PAD = LEAD - N_META
EPS = 1e-6

GLA_HEADS = 4
GLA_DK = D_MODEL // 2
GLA_DV = D_MODEL
GLA_HK = GLA_DK // GLA_HEADS
GLA_HV = GLA_DV // GLA_HEADS
GLA_RANK = 16
GLA_NORMALIZER = 16.0
GLA_CHUNK = 64

SWA_HQ = 16
SWA_HKV = 2
SWA_HD = 64
SWA_BLOCK = 128

SSD_DINNER = 2 * D_MODEL
SSD_HEADDIM = 64
SSD_HEADS = SSD_DINNER // SSD_HEADDIM
SSD_GROUPS = 4
SSD_DSTATE = 128
SSD_CONV = 4
SSD_CHUNK = 128
SSD_BC = SSD_GROUPS * SSD_DSTATE
SSD_XBC = SSD_DINNER + 2 * SSD_BC
SSD_GROUP_WIDTH = SSD_DINNER // SSD_GROUPS

N_BRANCH = 3
D_FF = 4 * D_MODEL

LANES = 128
SUBLANES = 8

SEG_GATE, SEG_XBC, SEG_Z, SEG_GV, SEG_GR, SEG_SQ, SEG_GQ, SEG_GK, SEG_SK, SEG_SV = range(10)
_SEG_WIDTHS = (N_BRANCH * D_MODEL, SSD_XBC, SSD_DINNER, GLA_DV, GLA_DV,
               SWA_HQ * SWA_HD, GLA_DK, GLA_DK, SWA_HKV * SWA_HD, SWA_HKV * SWA_HD)
_SEG_STARTS = tuple(sum(_SEG_WIDTHS[:i]) for i in range(len(_SEG_WIDTHS)))
N_BIG = sum(_SEG_WIDTHS)
assert all(s % w == 0 for s, w in zip(_SEG_STARTS, _SEG_WIDTHS))
SMALL_DT = 0
SMALL_LR = SSD_HEADS
N_SMALL = LANES

PROJ_TM = 1024
PROJ_TN = 1792
ROW_TILE = 384
GLA_ROWS = 128
VMEM_LIMIT = 56 * 1024 * 1024


def _seg_block(seg):
    return _SEG_STARTS[seg] // _SEG_WIDTHS[seg]


def _nt(a, b):
    return lax.dot_general(a, b, (((1,), (1,)), ((), ())), preferred_element_type=F32)


def _tn(a, b):
    return lax.dot_general(a, b, (((0,), (0,)), ((), ())), preferred_element_type=F32)


def _split_dot(m01, x):
    hi = x.astype(BF16)
    r1 = x - hi.astype(F32)
    mid = r1.astype(BF16)
    lo = (r1 - mid.astype(F32)).astype(BF16)
    return (jnp.dot(m01, hi, preferred_element_type=F32)
            + jnp.dot(m01, mid, preferred_element_type=F32)
            + jnp.dot(m01, lo, preferred_element_type=F32))


def _tril_ones(n):
    r = lax.broadcasted_iota(jnp.int32, (n, n), 0)
    c = lax.broadcasted_iota(jnp.int32, (n, n), 1)
    return r >= c


def _log1p_exp_neg_abs(x):
    u = jnp.exp(-jnp.abs(x))
    w = 1.0 + u
    d = w - 1.0
    return jnp.where(d == 0.0, u, jnp.log(w) * (u / d))


def _softplus(x):
    return jnp.maximum(x, 0.0) + _log1p_exp_neg_abs(x)


def _log_sigmoid(x):
    return jnp.minimum(x, 0.0) - _log1p_exp_neg_abs(x)


def _silu(x):
    return x * jax.nn.sigmoid(x)


def _rms(x, g):
    return x * lax.rsqrt(jnp.mean(x * x, axis=-1, keepdims=True) + EPS) * g


def _norm_kernel(x_ref, g_ref, o_ref):
    o_ref[0] = _rms(x_ref[0], g_ref[...]).astype(o_ref.dtype)


def _norm(x, g):
    bsz, L, d = x.shape
    return pl.pallas_call(
        _norm_kernel,
        grid=(bsz, L // ROW_TILE),
        in_specs=[pl.BlockSpec((1, ROW_TILE, d), lambda b, j: (b, j, 0)),
                  pl.BlockSpec((1, d), lambda b, j: (0, 0))],
        out_specs=pl.BlockSpec((1, ROW_TILE, d), lambda b, j: (b, j, 0)),
        out_shape=jax.ShapeDtypeStruct((bsz, L, d), BF16),
        name="first_norm",
    )(x, g)


def _inproj_kernel(h_ref, w_ref, ws_ref, o_ref, os_ref):
    o_ref[...] = jnp.dot(h_ref[...], w_ref[...], preferred_element_type=F32).astype(o_ref.dtype)

    @pl.when(pl.program_id(1) == 0)
    def _():
        os_ref[...] = jnp.dot(h_ref[...], ws_ref[...], preferred_element_type=F32)


def _inproj(hn, w_big, w_small):
    T, d = hn.shape
    return pl.pallas_call(
        _inproj_kernel,
        grid=(T // PROJ_TM, N_BIG // PROJ_TN),
        in_specs=[pl.BlockSpec((PROJ_TM, d), lambda i, j: (i, 0)),
                  pl.BlockSpec((d, PROJ_TN), lambda i, j: (0, j)),
                  pl.BlockSpec((d, N_SMALL), lambda i, j: (0, 0))],
        out_specs=[pl.BlockSpec((PROJ_TM, PROJ_TN), lambda i, j: (i, j)),
                   pl.BlockSpec((PROJ_TM, N_SMALL), lambda i, j: (i, 0))],
        out_shape=[jax.ShapeDtypeStruct((T, N_BIG), BF16),
                   jax.ShapeDtypeStruct((T, N_SMALL), F32)],
        compiler_params=pltpu.CompilerParams(
            dimension_semantics=("arbitrary", "arbitrary"), vmem_limit_bytes=VMEM_LIMIT),
        name="in_proj",
    )(hn, w_big, w_small)


def _gla_kernel(q_ref, k_ref, v_ref, r_ref, sm_ref, wg_ref, bg_ref, nw_ref, o_ref, st_ref):
    @pl.when(pl.program_id(1) == 0)
    def _():
        st_ref[...] = jnp.zeros_like(st_ref)

    logits = jnp.dot(sm_ref[0].astype(BF16), wg_ref[...], preferred_element_type=F32) + bg_ref[...]
    gk_all = _log_sigmoid(logits) * (1.0 / GLA_NORMALIZER)
    causal = _tril_ones(GLA_CHUNK)
    tril = causal.astype(BF16)
    for c in range(GLA_ROWS // GLA_CHUNK):
        rows = slice(c * GLA_CHUNK, (c + 1) * GLA_CHUNK)
        g_all = _split_dot(tril, gk_all[rows])
        for h in range(GLA_HEADS):
            hk = slice(h * GLA_HK, (h + 1) * GLA_HK)
            hv = slice(h * GLA_HV, (h + 1) * GLA_HV)
            g = g_all[:, hk]
            g_last = g[GLA_CHUNK - 1:GLA_CHUNK, :]
            q = q_ref[0, rows, hk].astype(F32) * (GLA_HK ** -0.5)
            k = k_ref[0, rows, hk].astype(F32)
            v = v_ref[0, rows, hv]
            q_dec = (q * jnp.exp(g)).astype(BF16)
            k_inv = (k * jnp.exp(-g)).astype(BF16)
            k_end = (k * jnp.exp(g_last - g)).astype(BF16)
            attn = jnp.where(causal, _nt(q_dec, k_inv), 0.0).astype(BF16)
            state_t = st_ref[h]
            o = jnp.dot(attn, v, preferred_element_type=F32) + _nt(q_dec, state_t.astype(BF16))
            st_ref[h] = state_t * jnp.exp(g_last) + _tn(v, k_end)
            o = o * lax.rsqrt(jnp.mean(o * o, axis=-1, keepdims=True) + EPS) * nw_ref[...]
            o = o * _silu(r_ref[0, rows, hv].astype(F32))
            o_ref[0, rows, hv] = o.astype(o_ref.dtype)


def _gla(proj, small, wg, bg, nw):
    bsz, L, _ = proj.shape
    row = lambda seg: (lambda b, n: (b, n, _seg_block(seg)))
    const = lambda b, n: (0, 0)
    return pl.pallas_call(
        _gla_kernel,
        grid=(bsz, L // GLA_ROWS),
        in_specs=[pl.BlockSpec((1, GLA_ROWS, GLA_DK), row(SEG_GQ)),
                  pl.BlockSpec((1, GLA_ROWS, GLA_DK), row(SEG_GK)),
                  pl.BlockSpec((1, GLA_ROWS, GLA_DV), row(SEG_GV)),
                  pl.BlockSpec((1, GLA_ROWS, GLA_DV), row(SEG_GR)),
                  pl.BlockSpec((1, GLA_ROWS, N_SMALL), lambda b, n: (b, n, 0)),
                  pl.BlockSpec((N_SMALL, GLA_DK), const),
                  pl.BlockSpec((1, GLA_DK), const),
                  pl.BlockSpec((1, GLA_HV), const)],
        out_specs=pl.BlockSpec((1, GLA_ROWS, GLA_DV), lambda b, n: (b, n, 0)),
        out_shape=jax.ShapeDtypeStruct((bsz, L, GLA_DV), BF16),
        scratch_shapes=[pltpu.VMEM((GLA_HEADS, GLA_HV, GLA_HK), F32)],
        compiler_params=pltpu.CompilerParams(
            dimension_semantics=("arbitrary", "arbitrary"), vmem_limit_bytes=VMEM_LIMIT),
        name="gla",
    )(proj, proj, proj, proj, small, wg, bg, nw)


def _swa_kernel(q_ref, kp_ref, kc_ref, vp_ref, vc_ref, sink_ref, o_ref):
    n = pl.program_id(1)
    nk = 2 * SWA_BLOCK
    low_q = lax.broadcasted_iota(jnp.int32, (SWA_BLOCK, LANES), 1) < SWA_HD
    low_k = lax.broadcasted_iota(jnp.int32, (nk, LANES), 1) < SWA_HD

    def per_kv_head(prev_ref, cur_ref):
        both = jnp.concatenate([prev_ref[0], cur_ref[0]], axis=0).astype(F32)
        swapped = pltpu.roll(both, SWA_HD, axis=1)
        return (jnp.where(low_k, both, swapped).astype(BF16),
                jnp.where(low_k, swapped, both).astype(BF16))

    k_ext = per_kv_head(kp_ref, kc_ref)
    both_v = jnp.concatenate([vp_ref[0], vc_ref[0]], axis=0).astype(F32)
    swapped_v = pltpu.roll(both_v, SWA_HD, axis=1)
    v_ones = []
    for in_low, in_high in ((both_v, swapped_v), (swapped_v, both_v)):
        v_ones.append((jnp.where(low_k, in_low, 1.0).astype(BF16),
                       jnp.where(low_k, 1.0, in_high).astype(BF16)))
    qi = lax.broadcasted_iota(jnp.int32, (SWA_BLOCK, nk), 0)
    kk = lax.broadcasted_iota(jnp.int32, (SWA_BLOCK, nk), 1)
    allowed = (kk > qi) & (kk <= qi + SWA_BLOCK) & (kk >= PAD + SWA_BLOCK - SWA_BLOCK * n)
    heads_per_kv = SWA_HQ // SWA_HKV
    for j in range(SWA_HQ // 2):
        hkv = (2 * j) // heads_per_kv
        cols = slice(j * LANES, (j + 1) * LANES)
        q_pair = q_ref[0, :, cols] * (SWA_HD ** -0.5)
        full, sink_term = [], []
        for half in range(2):
            q_half = jnp.where(low_q if half == 0 else ~low_q, q_pair, jnp.zeros_like(q_pair))
            s = jnp.where(allowed, _nt(q_half, k_ext[hkv]), -jnp.inf)
            sink = sink_ref[2 * j + half]
            m = jnp.maximum(jnp.max(s, axis=-1, keepdims=True), sink)
            p = jnp.exp(s - m).astype(BF16)
            full.append(jnp.dot(p, v_ones[hkv][half], preferred_element_type=F32))
            sink_term.append(jnp.exp(sink - m))
        num = jnp.where(low_q, full[0], full[1])
        sums = pltpu.roll(jnp.where(low_q, full[1], full[0]), SWA_HD, axis=1)
        den = sums + jnp.where(low_q, sink_term[0], sink_term[1])
        o_ref[0, :, cols] = (num * (1.0 / den)).astype(o_ref.dtype)


def _swa(proj, sinks):
    bsz, L, _ = proj.shape
    kvw = SWA_HKV * SWA_HD
    cur = lambda seg: (lambda b, n: (b, n, _seg_block(seg)))
    prev = lambda seg: (lambda b, n: (b, jnp.maximum(n - 1, 0), _seg_block(seg)))
    return pl.pallas_call(
        _swa_kernel,
        grid=(bsz, L // SWA_BLOCK),
        in_specs=[pl.BlockSpec((1, SWA_BLOCK, SWA_HQ * SWA_HD), cur(SEG_SQ)),
                  pl.BlockSpec((1, SWA_BLOCK, kvw), prev(SEG_SK)),
                  pl.BlockSpec((1, SWA_BLOCK, kvw), cur(SEG_SK)),
                  pl.BlockSpec((1, SWA_BLOCK, kvw), prev(SEG_SV)),
                  pl.BlockSpec((1, SWA_BLOCK, kvw), cur(SEG_SV)),
                  pl.BlockSpec(memory_space=pltpu.SMEM)],
        out_specs=pl.BlockSpec((1, SWA_BLOCK, SWA_HQ * SWA_HD), lambda b, n: (b, n, 0)),
        out_shape=jax.ShapeDtypeStruct((bsz, L, SWA_HQ * SWA_HD), BF16),
        compiler_params=pltpu.CompilerParams(
            dimension_semantics=("arbitrary", "arbitrary"), vmem_limit_bytes=VMEM_LIMIT),
        name="swa",
    )(proj, proj, proj, proj, proj, sinks)


def _ssd_block(n, z_ref, xbc_ref, sm_ref, cw_ref, cb_ref, dtb_ref, alog_ref, dskip_ref, nw_ref, shift_ref,
               o_ref, ext_ref, xs_ref, y_ref, st_ref, after_group):
    rows = SSD_CHUNK

    ext_ref[rows:2 * rows, :] = xbc_ref[0]
    col_chunk = 512
    bc_parts = []
    for c in range(SSD_XBC // col_chunk):
        cols = slice(c * col_chunk, (c + 1) * col_chunk)
        shifted = jnp.dot(shift_ref[...], ext_ref[:, cols], preferred_element_type=F32)
        acc = cb_ref[:, cols] + cw_ref[SSD_CONV - 1:SSD_CONV, cols] * xbc_ref[0, :, cols].astype(F32)
        for back in range(1, SSD_CONV):
            tap = SSD_CONV - 1 - back
            acc = acc + cw_ref[tap:tap + 1, cols] * shifted[(back - 1) * rows:back * rows, :]
        act = _silu(acc)
        if c < SSD_DINNER // col_chunk:
            xs_ref[:, cols] = act
        else:
            bc_parts.append(act.astype(BF16))
    ext_ref[0:rows, :] = xbc_ref[0]
    b_all, c_all = bc_parts

    lane = lax.broadcasted_iota(jnp.int32, (1, LANES), 1)
    pos = n * rows + lax.broadcasted_iota(jnp.int32, (rows, 1), 0)
    valid = (pos >= PAD).astype(F32)
    dt = _softplus(sm_ref[0] + dtb_ref[...]) * valid
    a_coef = jnp.where(lane < SSD_HEADS, -jnp.exp(alog_ref[...]), 0.0)
    causal = _tril_ones(rows)
    a_cs = _split_dot(causal.astype(BF16), dt * a_coef)
    a_cs_t = a_cs.T
    a_last_t = a_cs_t[:, rows - 1:rows]
    dt_t = dt.T
    w_end_t = dt_t * jnp.exp(a_last_t - a_cs_t)
    c_decay_t = jnp.exp(a_last_t)

    low = lax.broadcasted_iota(jnp.int32, (rows, LANES), 1) < SSD_HEADDIM
    top = lax.broadcasted_iota(jnp.int32, (LANES, 1), 0) < SSD_HEADDIM

    def col(x, h):
        return jnp.broadcast_to(x[:, h:h + 1], (rows, LANES))

    pairs_per_group = SSD_GROUP_WIDTH // LANES
    for g in range(SSD_GROUPS):
        gs = slice(g * SSD_DSTATE, (g + 1) * SSD_DSTATE)
        b_g = b_all[:, gs]
        c_g = c_all[:, gs]
        cb = _nt(c_g, b_g)
        srows = slice(g * SSD_GROUP_WIDTH, (g + 1) * SSD_GROUP_WIDTH)
        y_off = _nt(c_g, st_ref[srows, :].astype(BF16))
        ssq = jnp.zeros((rows, 1), F32)
        x_end_t, decay = [], []
        for jj in range(pairs_per_group):
            j = g * pairs_per_group + jj
            ha, hb = 2 * j, 2 * j + 1
            cols = slice(j * LANES, (j + 1) * LANES)
            xs = xs_ref[:, cols]
            xs_bf = xs.astype(BF16)
            a_col = (col(a_cs, ha), col(a_cs, hb))
            m_pair = []
            for h, a_c in ((ha, a_col[0]), (hb, a_col[1])):
                seg = a_c - a_cs_t[h:h + 1, :]
                m_h = cb * jnp.exp(jnp.where(causal, seg, -jnp.inf)) * dt_t[h:h + 1, :]
                m_pair.append(m_h.astype(BF16))
            zero = jnp.zeros_like(xs_bf)
            xs_split = jnp.concatenate([jnp.where(low, xs_bf, zero), jnp.where(low, zero, xs_bf)], axis=0)
            y = jnp.dot(jnp.concatenate(m_pair, axis=1), xs_split, preferred_element_type=F32)
            y = y + y_off[:, jj * LANES:(jj + 1) * LANES] * jnp.exp(jnp.where(low, a_col[0], a_col[1]))
            y = y + dskip_ref[:, cols] * xs
            w_rows = jnp.where(top, w_end_t[ha:ha + 1, :], w_end_t[hb:hb + 1, :])
            x_end_t.append((xs.T * w_rows).astype(BF16))
            decay.append(jnp.where(top, c_decay_t[ha:ha + 1, :], c_decay_t[hb:hb + 1, :]))
            y = y * _silu(z_ref[0, :, cols].astype(F32))
            ssq = ssq + jnp.sum(y * y, axis=-1, keepdims=True)
            y_ref[:, cols] = y
        st_ref[srows, :] = (st_ref[srows, :] * jnp.concatenate(decay, axis=0)
                            + jnp.dot(jnp.concatenate(x_end_t, axis=0), b_g, preferred_element_type=F32))
        scale = lax.rsqrt(ssq * (1.0 / SSD_GROUP_WIDTH) + EPS)
        gcols = slice(g * SSD_GROUP_WIDTH, (g + 1) * SSD_GROUP_WIDTH)
        o_ref[:, gcols] = (y_ref[:, gcols] * scale * nw_ref[:, gcols]).astype(o_ref.dtype)
        after_group(g)


def _merge_block(j, ga_ref, sw_ref, sd_ref, gl_ref, x_ref, wa_ref, wb_ref, wc_ref, wo_ref, bg_ref):
    rows = SSD_CHUNK
    pos = j * rows + lax.broadcasted_iota(jnp.int32, (rows, 1), 0)
    valid = (pos >= PAD).astype(F32)
    gates = jax.nn.sigmoid(gl_ref[0].astype(F32) + bg_ref[...])
    merged = gates[:, 0:D_MODEL] * jnp.dot(ga_ref[0], wa_ref[...], preferred_element_type=F32)
    merged += gates[:, D_MODEL:2 * D_MODEL] * jnp.dot(sw_ref[0], wb_ref[...], preferred_element_type=F32)
    merged += gates[:, 2 * D_MODEL:] * jnp.dot(sd_ref[...], wc_ref[...], preferred_element_type=F32)
    x = x_ref[0] + jnp.dot(merged.astype(BF16), wo_ref[...], preferred_element_type=F32) * valid
    return x, valid


def _mlp_chunk(c, h2, w1_ref, w2_ref):
    ff_chunk = D_FF // SSD_GROUPS
    cols = slice(c * ff_chunk, (c + 1) * ff_chunk)
    hid = jnp.dot(h2, w1_ref[:, cols], preferred_element_type=F32)
    hid = jnp.square(jnp.maximum(hid, 0.0)).astype(BF16)
    return jnp.dot(hid, w2_ref[cols, :], preferred_element_type=F32)


def _tail_kernel(z_ref, xbc_ref, sm_ref, cw_ref, cb_ref, dtb_ref, alog_ref, dskip_ref, nw_ref, shift_ref,
                 ga_ref, sw_ref, gl_ref, x_ref, wa_ref, wb_ref, wc_ref, wo_ref, bg_ref,
                 nm_ref, w1_ref, w2_ref, gn_ref,
                 xo_ref, hn_ref,
                 ext_ref, xs_ref, y_ref, st_ref, cur_ref, prev_ref):
    n = pl.program_id(1)

    @pl.when(n == 0)
    def _():
        ext_ref[0:SSD_CHUNK, :] = jnp.zeros((SSD_CHUNK, SSD_XBC), BF16)
        st_ref[...] = jnp.zeros_like(st_ref)
        cur_ref[...] = jnp.zeros_like(cur_ref)

    prev_ref[...] = cur_ref[...]
    x, valid = _merge_block(n - 1, ga_ref, sw_ref, prev_ref, gl_ref, x_ref, wa_ref, wb_ref, wc_ref, wo_ref,
                            bg_ref)
    h2 = _rms(x, nm_ref[...]).astype(BF16)
    mlp_out = [jnp.zeros_like(x)]

    def mlp_chunk(c):
        mlp_out[0] = mlp_out[0] + _mlp_chunk(c, h2, w1_ref, w2_ref)

    _ssd_block(n, z_ref, xbc_ref, sm_ref, cw_ref, cb_ref, dtb_ref, alog_ref, dskip_ref, nw_ref, shift_ref,
               cur_ref, ext_ref, xs_ref, y_ref, st_ref, after_group=mlp_chunk)
    xo = x + mlp_out[0] * valid
    xo_ref[0] = xo
    hn_ref[0] = _rms(xo, gn_ref[...]).astype(hn_ref.dtype)


def _tail(proj, small, y_gla, y_swa, x, ssd_params, merge_params, mlp_params, hn_dtype):
    bsz, L, d = x.shape
    nb = L // SSD_CHUNK
    cur = lambda seg: (lambda b, n: (b, jnp.minimum(n, nb - 1), seg))
    lag = lambda seg: (lambda b, n: (b, jnp.maximum(n - 1, 0), seg))
    const = lambda b, n: (0, 0)
    single = pl.Buffered(1)
    conv_w, conv_b, dt_bias, a_log, d_exp, ssd_nw = ssd_params
    wa, wb, wc, wo, b_gate = merge_params
    norm_w, w1, w2, next_norm_w = mlp_params
    return pl.pallas_call(
        _tail_kernel,
        grid=(bsz, nb + 1),
        in_specs=[pl.BlockSpec((1, SSD_CHUNK, SSD_DINNER), cur(_seg_block(SEG_Z))),
                  pl.BlockSpec((1, SSD_CHUNK, SSD_XBC), cur(_seg_block(SEG_XBC))),
                  pl.BlockSpec((1, SSD_CHUNK, N_SMALL), cur(0)),
                  pl.BlockSpec((SSD_CONV, SSD_XBC), const),
                  pl.BlockSpec((1, SSD_XBC), const),
                  pl.BlockSpec((1, N_SMALL), const),
                  pl.BlockSpec((1, N_SMALL), const),
                  pl.BlockSpec((1, SSD_DINNER), const),
                  pl.BlockSpec((1, SSD_DINNER), const),
                  pl.BlockSpec(((SSD_CONV - 1) * SSD_CHUNK, 2 * SSD_CHUNK), const),
                  pl.BlockSpec((1, SSD_CHUNK, GLA_DV), lag(0)),
                  pl.BlockSpec((1, SSD_CHUNK, SWA_HQ * SWA_HD), lag(0)),
                  pl.BlockSpec((1, SSD_CHUNK, N_BRANCH * D_MODEL), lag(_seg_block(SEG_GATE))),
                  pl.BlockSpec((1, SSD_CHUNK, d), lag(0)),
                  pl.BlockSpec((GLA_DV, d), const, pipeline_mode=single),
                  pl.BlockSpec((SWA_HQ * SWA_HD, d), const, pipeline_mode=single),
                  pl.BlockSpec((SSD_DINNER, d), const, pipeline_mode=single),
                  pl.BlockSpec((d, d), const, pipeline_mode=single),
                  pl.BlockSpec((1, N_BRANCH * D_MODEL), const),
                  pl.BlockSpec((1, d), const),
                  pl.BlockSpec((d, D_FF), const, pipeline_mode=single),
                  pl.BlockSpec((D_FF, d), const, pipeline_mode=single),
                  pl.BlockSpec((1, d), const)],
        out_specs=[pl.BlockSpec((1, SSD_CHUNK, d), lag(0)),
                   pl.BlockSpec((1, SSD_CHUNK, d), lag(0))],
        out_shape=[jax.ShapeDtypeStruct((bsz, L, d), F32),
                   jax.ShapeDtypeStruct((bsz, L, d), hn_dtype)],
        scratch_shapes=[pltpu.VMEM((2 * SSD_CHUNK, SSD_XBC), BF16),
                        pltpu.VMEM((SSD_CHUNK, SSD_DINNER), F32),
                        pltpu.VMEM((SSD_CHUNK, SSD_DINNER), F32),
                        pltpu.VMEM((SSD_DINNER, SSD_DSTATE), F32),
                        pltpu.VMEM((SSD_CHUNK, SSD_DINNER), BF16),
                        pltpu.VMEM((SSD_CHUNK, SSD_DINNER), BF16)],
        compiler_params=pltpu.CompilerParams(
            dimension_semantics=("arbitrary", "arbitrary"), vmem_limit_bytes=VMEM_LIMIT),
        name="ssd_merge_mlp",
    )(proj, proj, small, conv_w, conv_b, dt_bias, a_log, d_exp, ssd_nw, _conv_shift_matrix(),
      y_gla, y_swa, proj, x, wa, wb, wc, wo, b_gate, norm_w, w1, w2, next_norm_w)


def _conv_shift_matrix():
    r = jnp.arange((SSD_CONV - 1) * SSD_CHUNK)
    source = SSD_CHUNK + r % SSD_CHUNK - (r // SSD_CHUNK + 1)
    return (jnp.arange(2 * SSD_CHUNK)[None, :] == source[:, None]).astype(BF16)


def _split_w_in(w_in):
    sizes = (GLA_DK, GLA_DK, GLA_DV, GLA_DV, GLA_RANK, SWA_HQ * SWA_HD, SWA_HKV * SWA_HD,
             SWA_HKV * SWA_HD, SSD_DINNER, SSD_XBC, SSD_HEADS, N_BRANCH * D_MODEL)
    starts = [sum(sizes[:i]) for i in range(len(sizes))]
    part = lambda i: w_in[:, :, starts[i]:starts[i] + sizes[i]]
    g_q, g_k, g_v, g_r, g_lr, s_q, s_k, s_v, m_z, m_xbc, m_dt, gate = (part(i) for i in range(len(sizes)))
    big = jnp.concatenate([gate, m_xbc, m_z, g_v, g_r, s_q, g_q, g_k, s_k, s_v], axis=-1).astype(BF16)
    pad = jnp.zeros(w_in.shape[:2] + (N_SMALL - SSD_HEADS - GLA_RANK,), w_in.dtype)
    small = jnp.concatenate([m_dt, g_lr, pad], axis=-1).astype(BF16)
    return big, small


def _pad_lanes(v, offset=0):
    out = jnp.zeros((v.shape[0], 1, LANES), F32)
    return out.at[:, 0, offset:offset + v.shape[1]].set(v.astype(F32))


def kernel(x, meta_tokens, norm_mix, w_in, b_gate, gla_w_gate_up, gla_b_gate_up, gla_norm, swa_sinks,
           ssd_conv_w, ssd_conv_b, ssd_dt_bias, ssd_A_log, ssd_D, ssd_norm,
           w_branch_gla, w_branch_swa, w_branch_ssd, w_out, norm_mlp, w_mlp_in, w_mlp_out, final_norm):
    bsz, seq, d = x.shape
    L = LEAD + seq
    h = jnp.concatenate([jnp.zeros((bsz, PAD, d), x.dtype),
                         jnp.broadcast_to(meta_tokens.astype(x.dtype)[None], (bsz, N_META, d)),
                         x], axis=1)

    w_big, w_small = _split_w_in(w_in)
    wg = jnp.zeros((DEPTH, N_SMALL, GLA_DK), F32).at[:, SMALL_LR:SMALL_LR + GLA_RANK, :].set(gla_w_gate_up)
    wg = wg.astype(BF16)
    dt_bias = _pad_lanes(ssd_dt_bias, SMALL_DT)
    a_log = _pad_lanes(ssd_A_log, SMALL_DT)
    d_exp = jnp.repeat(ssd_D.astype(F32), SSD_HEADDIM, axis=-1)[:, None, :]
    wa, wb, wc, wo = (w.astype(BF16) for w in (w_branch_gla, w_branch_swa, w_branch_ssd, w_out))
    w1, w2 = w_mlp_in.astype(BF16), w_mlp_out.astype(BF16)
    row = lambda p: p.astype(F32)[:, None, :]
    norm_mix_r, norm_mlp_r, b_gate_r = row(norm_mix), row(norm_mlp), row(b_gate)
    gla_bg_r, gla_nw_r = row(gla_b_gate_up), row(gla_norm)
    conv_b_r, ssd_nw_r = row(ssd_conv_b), row(ssd_norm)
    final_r = final_norm.astype(F32)[None, :]

    hn = _norm(h, norm_mix_r[0])
    for i in range(DEPTH):
        proj, small = _inproj(hn.reshape(bsz * L, d), w_big[i], w_small[i])
        proj = proj.reshape(bsz, L, N_BIG)
        small = small.reshape(bsz, L, N_SMALL)
        y_gla = _gla(proj, small, wg[i], gla_bg_r[i], gla_nw_r[i])
        y_swa = _swa(proj, swa_sinks[i].astype(F32))
        last = i == DEPTH - 1
        next_g = final_r if last else norm_mix_r[i + 1]
        h, hn = _tail(proj, small, y_gla, y_swa, h,
                      (ssd_conv_w[i].astype(F32), conv_b_r[i], dt_bias[i], a_log[i], d_exp[i], ssd_nw_r[i]),
                      (wa[i], wb[i], wc[i], wo[i], b_gate_r[i]),
                      (norm_mlp_r[i], w1[i], w2[i], next_g), F32 if last else BF16)
    return hn[:, LEAD:, :]
```

```python
import functools

import jax
import jax.numpy as jnp
from jax import lax
from jax.experimental import pallas as pl
from jax.experimental.pallas import tpu as pltpu

F32 = jnp.float32
BF16 = jnp.bfloat16

D_MODEL = 1024
DEPTH = 4
N_META = 16
LEAD = 128
PAD = LEAD - N_META
EPS = 1e-6

GLA_HEADS = 4
GLA_DK = D_MODEL // 2
GLA_DV = D_MODEL
GLA_HK = GLA_DK // GLA_HEADS
GLA_HV = GLA_DV // GLA_HEADS
GLA_RANK = 16
GLA_NORMALIZER = 16.0
GLA_CHUNK = 64

SWA_HQ = 16
SWA_HKV = 2
SWA_HD = 64
SWA_BLOCK = 128

SSD_DINNER = 2 * D_MODEL
SSD_HEADDIM = 64
SSD_HEADS = SSD_DINNER // SSD_HEADDIM
SSD_GROUPS = 4
SSD_DSTATE = 128
SSD_CONV = 4
SSD_CHUNK = 128
SSD_BC = SSD_GROUPS * SSD_DSTATE
SSD_XBC = SSD_DINNER + 2 * SSD_BC
SSD_GROUP_WIDTH = SSD_DINNER // SSD_GROUPS

N_BRANCH = 3
D_FF = 4 * D_MODEL

LANES = 128
SUBLANES = 8

SEG_GATE, SEG_XBC, SEG_Z, SEG_GV, SEG_GR, SEG_SQ, SEG_GQ, SEG_GK, SEG_SK, SEG_SV = range(10)
_SEG_WIDTHS = (N_BRANCH * D_MODEL, SSD_XBC, SSD_DINNER, GLA_DV, GLA_DV,
               SWA_HQ * SWA_HD, GLA_DK, GLA_DK, SWA_HKV * SWA_HD, SWA_HKV * SWA_HD)
_SEG_STARTS = tuple(sum(_SEG_WIDTHS[:i]) for i in range(len(_SEG_WIDTHS)))
N_BIG = sum(_SEG_WIDTHS)
assert all(s % w == 0 for s, w in zip(_SEG_STARTS, _SEG_WIDTHS))
SMALL_DT = 0
SMALL_LR = SSD_HEADS
N_SMALL = LANES

PROJ_TM = 1024
PROJ_TN = 1792
GLA_ROWS = 128
VMEM_LIMIT = 56 * 1024 * 1024


def _seg_block(seg):
    return _SEG_STARTS[seg] // _SEG_WIDTHS[seg]


def _nt(a, b):
    return lax.dot_general(a, b, (((1,), (1,)), ((), ())), preferred_element_type=F32)


def _tn(a, b):
    return lax.dot_general(a, b, (((0,), (0,)), ((), ())), preferred_element_type=F32)


def _split_dot(m01, x):
    hi = x.astype(BF16)
    r1 = x - hi.astype(F32)
    mid = r1.astype(BF16)
    lo = (r1 - mid.astype(F32)).astype(BF16)
    return (jnp.dot(m01, hi, preferred_element_type=F32)
            + jnp.dot(m01, mid, preferred_element_type=F32)
            + jnp.dot(m01, lo, preferred_element_type=F32))


def _tril_ones(n):
    r = lax.broadcasted_iota(jnp.int32, (n, n), 0)
    c = lax.broadcasted_iota(jnp.int32, (n, n), 1)
    return r >= c


def _log1p_exp_neg_abs(x):
    u = jnp.exp(-jnp.abs(x))
    w = 1.0 + u
    d = w - 1.0
    return jnp.where(d == 0.0, u, jnp.log(w) * (u / d))


def _softplus(x):
    return jnp.maximum(x, 0.0) + _log1p_exp_neg_abs(x)


def _log_sigmoid(x):
    return jnp.minimum(x, 0.0) - _log1p_exp_neg_abs(x)


def _silu(x):
    return x * jax.nn.sigmoid(x)


def _rms(x, g):
    return x * lax.rsqrt(jnp.mean(x * x, axis=-1, keepdims=True) + EPS) * g


def _embed_kernel(x_ref, meta_ref, g_ref, h_ref, hn_ref):
    j = pl.program_id(1)

    def emit(block):
        h_ref[0] = block
        hn_ref[0] = _rms(block, g_ref[...]).astype(hn_ref.dtype)

    @pl.when(j == 0)
    def _():
        emit(jnp.concatenate([jnp.zeros((PAD, D_MODEL), F32), meta_ref[...]], axis=0))

    @pl.when(j > 0)
    def _():
        emit(x_ref[0])


def _embed(x, meta_tokens, g):
    bsz, seq, d = x.shape
    L = LEAD + seq
    row = lambda b, j: (b, j, 0)
    return pl.pallas_call(
        _embed_kernel,
        grid=(bsz, L // LEAD),
        in_specs=[pl.BlockSpec((1, LEAD, d), lambda b, j: (b, jnp.maximum(j - 1, 0), 0)),
                  pl.BlockSpec((N_META, d), lambda b, j: (0, 0)),
                  pl.BlockSpec((1, d), lambda b, j: (0, 0))],
        out_specs=[pl.BlockSpec((1, LEAD, d), row), pl.BlockSpec((1, LEAD, d), row)],
        out_shape=[jax.ShapeDtypeStruct((bsz, L, d), F32), jax.ShapeDtypeStruct((bsz, L, d), BF16)],
        name="embed_norm",
    )(x, meta_tokens, g)


def _inproj_kernel(h_ref, w_ref, ws_ref, o_ref, os_ref):
    o_ref[...] = jnp.dot(h_ref[...], w_ref[...], preferred_element_type=F32).astype(o_ref.dtype)

    @pl.when(pl.program_id(1) == 0)
    def _():
        os_ref[...] = jnp.dot(h_ref[...], ws_ref[...], preferred_element_type=F32)


def _inproj(hn, w_big, w_small):
    T, d = hn.shape
    return pl.pallas_call(
        _inproj_kernel,
        grid=(T // PROJ_TM, N_BIG // PROJ_TN),
        in_specs=[pl.BlockSpec((PROJ_TM, d), lambda i, j: (i, 0)),
                  pl.BlockSpec((d, PROJ_TN), lambda i, j: (0, j)),
                  pl.BlockSpec((d, N_SMALL), lambda i, j: (0, 0))],
        out_specs=[pl.BlockSpec((PROJ_TM, PROJ_TN), lambda i, j: (i, j)),
                   pl.BlockSpec((PROJ_TM, N_SMALL), lambda i, j: (i, 0))],
        out_shape=[jax.ShapeDtypeStruct((T, N_BIG), BF16),
                   jax.ShapeDtypeStruct((T, N_SMALL), F32)],
        compiler_params=pltpu.CompilerParams(
            dimension_semantics=("arbitrary", "arbitrary"), vmem_limit_bytes=VMEM_LIMIT),
        name="in_proj",
    )(hn, w_big, w_small)


def _gla_units(q_ref, k_ref, v_ref, r_ref, sm_ref, wg_ref, bg_ref, nw_ref, o_ref, st_ref):
    logits = jnp.dot(sm_ref[0].astype(BF16), wg_ref[...], preferred_element_type=F32) + bg_ref[...]
    gk_all = _log_sigmoid(logits) * (1.0 / GLA_NORMALIZER)
    causal = _tril_ones(GLA_CHUNK)
    tril = causal.astype(BF16)

    def unit(rows, g_all, h):
        hk = slice(h * GLA_HK, (h + 1) * GLA_HK)
        hv = slice(h * GLA_HV, (h + 1) * GLA_HV)
        g = g_all[:, hk]
        g_last = g[GLA_CHUNK - 1:GLA_CHUNK, :]
        q = q_ref[0, rows, hk].astype(F32) * (GLA_HK ** -0.5)
        k = k_ref[0, rows, hk].astype(F32)
        v = v_ref[0, rows, hv]
        q_dec = (q * jnp.exp(g)).astype(BF16)
        k_inv = (k * jnp.exp(-g)).astype(BF16)
        k_end = (k * jnp.exp(g_last - g)).astype(BF16)
        attn = jnp.where(causal, _nt(q_dec, k_inv), 0.0).astype(BF16)
        state_t = st_ref[h]
        o = jnp.dot(attn, v, preferred_element_type=F32) + _nt(q_dec, state_t.astype(BF16))
        st_ref[h] = state_t * jnp.exp(g_last) + _tn(v, k_end)
        o = o * lax.rsqrt(jnp.mean(o * o, axis=-1, keepdims=True) + EPS) * nw_ref[...]
        o = o * _silu(r_ref[0, rows, hv].astype(F32))
        o_ref[0, rows, hv] = o.astype(o_ref.dtype)

    units = []
    for c in range(GLA_ROWS // GLA_CHUNK):
        rows = slice(c * GLA_CHUNK, (c + 1) * GLA_CHUNK)
        g_all = _split_dot(tril, gk_all[rows])
        units += [functools.partial(unit, rows, g_all, h) for h in range(GLA_HEADS)]
    return units


def _swa_units(n, q_ref, kp_ref, kc_ref, vp_ref, vc_ref, sink_ref, o_ref):
    nk = 2 * SWA_BLOCK
    low_q = lax.broadcasted_iota(jnp.int32, (SWA_BLOCK, LANES), 1) < SWA_HD
    low_k = lax.broadcasted_iota(jnp.int32, (nk, LANES), 1) < SWA_HD

    both_k = jnp.concatenate([kp_ref[0], kc_ref[0]], axis=0).astype(F32)
    swapped_k = pltpu.roll(both_k, SWA_HD, axis=1)
    k_ext = (jnp.where(low_k, both_k, swapped_k).astype(BF16), jnp.where(low_k, swapped_k, both_k).astype(BF16))
    both_v = jnp.concatenate([vp_ref[0], vc_ref[0]], axis=0).astype(F32)
    swapped_v = pltpu.roll(both_v, SWA_HD, axis=1)
    v_ones = []
    for in_low, in_high in ((both_v, swapped_v), (swapped_v, both_v)):
        v_ones.append((jnp.where(low_k, in_low, 1.0).astype(BF16),
                       jnp.where(low_k, 1.0, in_high).astype(BF16)))
    qi = lax.broadcasted_iota(jnp.int32, (SWA_BLOCK, nk), 0)
    kk = lax.broadcasted_iota(jnp.int32, (SWA_BLOCK, nk), 1)
    allowed = (kk > qi) & (kk <= qi + SWA_BLOCK) & (kk >= PAD + SWA_BLOCK - SWA_BLOCK * n)
    heads_per_kv = SWA_HQ // SWA_HKV

    def unit(j):
        hkv = (2 * j) // heads_per_kv
        cols = slice(j * LANES, (j + 1) * LANES)
        q_pair = q_ref[0, :, cols] * (SWA_HD ** -0.5)
        full, sink_term = [], []
        for half in range(2):
            q_half = jnp.where(low_q if half == 0 else ~low_q, q_pair, jnp.zeros_like(q_pair))
            s = jnp.where(allowed, _nt(q_half, k_ext[hkv]), -jnp.inf)
            sink = sink_ref[2 * j + half]
            m = jnp.maximum(jnp.max(s, axis=-1, keepdims=True), sink)
            p = jnp.exp(s - m).astype(BF16)
            full.append(jnp.dot(p, v_ones[hkv][half], preferred_element_type=F32))
            sink_term.append(jnp.exp(sink - m))
        num = jnp.where(low_q, full[0], full[1])
        sums = pltpu.roll(jnp.where(low_q, full[1], full[0]), SWA_HD, axis=1)
        den = sums + jnp.where(low_q, sink_term[0], sink_term[1])
        o_ref[0, :, cols] = (num * (1.0 / den)).astype(o_ref.dtype)

    return [functools.partial(unit, j) for j in range(SWA_HQ // 2)]


def _mix_kernel(gq_ref, gk_ref, gv_ref, gr_ref, sm_ref, wg_ref, bg_ref, nw_ref,
                sq_ref, kp_ref, kc_ref, vp_ref, vc_ref, sink_ref,
                gla_ref, swa_ref, st_ref):
    n = pl.program_id(1)

    @pl.when(n == 0)
    def _():
        st_ref[...] = jnp.zeros_like(st_ref)

    gla = _gla_units(gq_ref, gk_ref, gv_ref, gr_ref, sm_ref, wg_ref, bg_ref, nw_ref, gla_ref, st_ref)
    swa = _swa_units(n, sq_ref, kp_ref, kc_ref, vp_ref, vc_ref, sink_ref, swa_ref)
    for unit in gla + swa:
        unit()


def _mix(proj, small, wg, bg, nw, sinks):
    bsz, L, _ = proj.shape
    assert GLA_ROWS == SWA_BLOCK
    kvw = SWA_HKV * SWA_HD
    cur = lambda seg: (lambda b, n: (b, n, _seg_block(seg)))
    prev = lambda seg: (lambda b, n: (b, jnp.maximum(n - 1, 0), _seg_block(seg)))
    const = lambda b, n: (0, 0)
    rows = lambda b, n: (b, n, 0)
    return pl.pallas_call(
        _mix_kernel,
        grid=(bsz, L // GLA_ROWS),
        in_specs=[pl.BlockSpec((1, GLA_ROWS, GLA_DK), cur(SEG_GQ)),
                  pl.BlockSpec((1, GLA_ROWS, GLA_DK), cur(SEG_GK)),
                  pl.BlockSpec((1, GLA_ROWS, GLA_DV), cur(SEG_GV)),
                  pl.BlockSpec((1, GLA_ROWS, GLA_DV), cur(SEG_GR)),
                  pl.BlockSpec((1, GLA_ROWS, N_SMALL), rows),
                  pl.BlockSpec((N_SMALL, GLA_DK), const),
                  pl.BlockSpec((1, GLA_DK), const),
                  pl.BlockSpec((1, GLA_HV), const),
                  pl.BlockSpec((1, SWA_BLOCK, SWA_HQ * SWA_HD), cur(SEG_SQ)),
                  pl.BlockSpec((1, SWA_BLOCK, kvw), prev(SEG_SK)),
                  pl.BlockSpec((1, SWA_BLOCK, kvw), cur(SEG_SK)),
                  pl.BlockSpec((1, SWA_BLOCK, kvw), prev(SEG_SV)),
                  pl.BlockSpec((1, SWA_BLOCK, kvw), cur(SEG_SV)),
                  pl.BlockSpec(memory_space=pltpu.SMEM)],
        out_specs=[pl.BlockSpec((1, GLA_ROWS, GLA_DV), rows),
                   pl.BlockSpec((1, SWA_BLOCK, SWA_HQ * SWA_HD), rows)],
        out_shape=[jax.ShapeDtypeStruct((bsz, L, GLA_DV), BF16),
                   jax.ShapeDtypeStruct((bsz, L, SWA_HQ * SWA_HD), BF16)],
        scratch_shapes=[pltpu.VMEM((GLA_HEADS, GLA_HV, GLA_HK), F32)],
        compiler_params=pltpu.CompilerParams(
            dimension_semantics=("arbitrary", "arbitrary"), vmem_limit_bytes=VMEM_LIMIT),
        name="gla_swa",
    )(proj, proj, proj, proj, small, wg, bg, nw, proj, proj, proj, proj, proj, sinks)


def _ssd_block(n, z_ref, xbc_ref, sm_ref, cw_ref, cb_ref, dtb_ref, alog_ref, dskip_ref, nw_ref, shift_ref,
               o_ref, ext_ref, xs_ref, y_ref, st_ref, after_group):
    rows = SSD_CHUNK

    ext_ref[rows:2 * rows, :] = xbc_ref[0]
    col_chunk = 512
    bc_parts = []
    for c in range(SSD_XBC // col_chunk):
        cols = slice(c * col_chunk, (c + 1) * col_chunk)
        shifted = jnp.dot(shift_ref[...], ext_ref[:, cols], preferred_element_type=F32)
        acc = cb_ref[:, cols] + cw_ref[SSD_CONV - 1:SSD_CONV, cols] * xbc_ref[0, :, cols].astype(F32)
        for back in range(1, SSD_CONV):
            tap = SSD_CONV - 1 - back
            acc = acc + cw_ref[tap:tap + 1, cols] * shifted[(back - 1) * rows:back * rows, :]
        act = _silu(acc)
        if c < SSD_DINNER // col_chunk:
            xs_ref[:, cols] = act
        else:
            bc_parts.append(act.astype(BF16))
    ext_ref[0:rows, :] = xbc_ref[0]
    b_all, c_all = bc_parts

    lane = lax.broadcasted_iota(jnp.int32, (1, LANES), 1)
    pos = n * rows + lax.broadcasted_iota(jnp.int32, (rows, 1), 0)
    valid = (pos >= PAD).astype(F32)
    dt = _softplus(sm_ref[0] + dtb_ref[...]) * valid
    a_coef = jnp.where(lane < SSD_HEADS, -jnp.exp(alog_ref[...]), 0.0)
    causal = _tril_ones(rows)
    a_cs = _split_dot(causal.astype(BF16), dt * a_coef)
    a_cs_t = a_cs.T
    a_last_t = a_cs_t[:, rows - 1:rows]
    dt_t = dt.T
    w_end_t = dt_t * jnp.exp(a_last_t - a_cs_t)
    c_decay_t = jnp.exp(a_last_t)

    low = lax.broadcasted_iota(jnp.int32, (rows, LANES), 1) < SSD_HEADDIM
    top = lax.broadcasted_iota(jnp.int32, (LANES, 1), 0) < SSD_HEADDIM

    def col(x, h):
        return jnp.broadcast_to(x[:, h:h + 1], (rows, LANES))

    pairs_per_group = SSD_GROUP_WIDTH // LANES
    for g in range(SSD_GROUPS):
        gs = slice(g * SSD_DSTATE, (g + 1) * SSD_DSTATE)
        b_g = b_all[:, gs]
        c_g = c_all[:, gs]
        cb = _nt(c_g, b_g)
        srows = slice(g * SSD_GROUP_WIDTH, (g + 1) * SSD_GROUP_WIDTH)
        y_off = _nt(c_g, st_ref[srows, :].astype(BF16))
        ssq = jnp.zeros((rows, 1), F32)
        x_end_t, decay = [], []
        for jj in range(pairs_per_group):
            j = g * pairs_per_group + jj
            ha, hb = 2 * j, 2 * j + 1
            cols = slice(j * LANES, (j + 1) * LANES)
            xs = xs_ref[:, cols]
            xs_bf = xs.astype(BF16)
            a_col = (col(a_cs, ha), col(a_cs, hb))
            m_pair = []
            for h, a_c in ((ha, a_col[0]), (hb, a_col[1])):
                seg = a_c - a_cs_t[h:h + 1, :]
                m_h = cb * jnp.exp(jnp.where(causal, seg, -jnp.inf)) * dt_t[h:h + 1, :]
                m_pair.append(m_h.astype(BF16))
            zero = jnp.zeros_like(xs_bf)
            xs_split = jnp.concatenate([jnp.where(low, xs_bf, zero), jnp.where(low, zero, xs_bf)], axis=0)
            y = jnp.dot(jnp.concatenate(m_pair, axis=1), xs_split, preferred_element_type=F32)
            y = y + y_off[:, jj * LANES:(jj + 1) * LANES] * jnp.exp(jnp.where(low, a_col[0], a_col[1]))
            y = y + dskip_ref[:, cols] * xs
            w_rows = jnp.where(top, w_end_t[ha:ha + 1, :], w_end_t[hb:hb + 1, :])
            x_end_t.append((xs.T * w_rows).astype(BF16))
            decay.append(jnp.where(top, c_decay_t[ha:ha + 1, :], c_decay_t[hb:hb + 1, :]))
            y = y * _silu(z_ref[0, :, cols].astype(F32))
            ssq = ssq + jnp.sum(y * y, axis=-1, keepdims=True)
            y_ref[:, cols] = y
        st_ref[srows, :] = (st_ref[srows, :] * jnp.concatenate(decay, axis=0)
                            + jnp.dot(jnp.concatenate(x_end_t, axis=0), b_g, preferred_element_type=F32))
        scale = lax.rsqrt(ssq * (1.0 / SSD_GROUP_WIDTH) + EPS)
        gcols = slice(g * SSD_GROUP_WIDTH, (g + 1) * SSD_GROUP_WIDTH)
        o_ref[:, gcols] = (y_ref[:, gcols] * scale * nw_ref[:, gcols]).astype(o_ref.dtype)
        after_group(g)


def _merge_block(j, ga_ref, sw_ref, sd_ref, gl_ref, x_ref, wa_ref, wb_ref, wc_ref, wo_ref, bg_ref):
    rows = SSD_CHUNK
    pos = j * rows + lax.broadcasted_iota(jnp.int32, (rows, 1), 0)
    valid = (pos >= PAD).astype(F32)
    gates = jax.nn.sigmoid(gl_ref[0].astype(F32) + bg_ref[...])
    merged = gates[:, 0:D_MODEL] * jnp.dot(ga_ref[0], wa_ref[...], preferred_element_type=F32)
    merged += gates[:, D_MODEL:2 * D_MODEL] * jnp.dot(sw_ref[0], wb_ref[...], preferred_element_type=F32)
    merged += gates[:, 2 * D_MODEL:] * jnp.dot(sd_ref[...], wc_ref[...], preferred_element_type=F32)
    x = x_ref[0] + jnp.dot(merged.astype(BF16), wo_ref[...], preferred_element_type=F32) * valid
    return x, valid


def _mlp_chunk(c, h2, w1_ref, w2_ref):
    ff_chunk = D_FF // SSD_GROUPS
    cols = slice(c * ff_chunk, (c + 1) * ff_chunk)
    hid = jnp.dot(h2, w1_ref[:, cols], preferred_element_type=F32)
    hid = jnp.square(jnp.maximum(hid, 0.0)).astype(BF16)
    return jnp.dot(hid, w2_ref[cols, :], preferred_element_type=F32)


def _tail_kernel(z_ref, xbc_ref, sm_ref, cw_ref, cb_ref, dtb_ref, alog_ref, dskip_ref, nw_ref, shift_ref,
                 ga_ref, sw_ref, gl_ref, x_ref, wa_ref, wb_ref, wc_ref, wo_ref, bg_ref,
                 nm_ref, w1_ref, w2_ref, gn_ref, *outs_and_scratch, emit_stream):
    if emit_stream:
        xo_ref, hn_ref, ext_ref, xs_ref, y_ref, st_ref, cur_ref, prev_ref = outs_and_scratch
    else:
        hn_ref, ext_ref, xs_ref, y_ref, st_ref, cur_ref, prev_ref = outs_and_scratch
    n = pl.program_id(1)

    @pl.when(n == 0)
    def _():
        ext_ref[0:SSD_CHUNK, :] = jnp.zeros((SSD_CHUNK, SSD_XBC), BF16)
        st_ref[...] = jnp.zeros_like(st_ref)
        cur_ref[...] = jnp.zeros_like(cur_ref)

    prev_ref[...] = cur_ref[...]
    x, valid = _merge_block(n - 1, ga_ref, sw_ref, prev_ref, gl_ref, x_ref, wa_ref, wb_ref, wc_ref, wo_ref,
                            bg_ref)
    h2 = _rms(x, nm_ref[...]).astype(BF16)
    mlp_out = [jnp.zeros_like(x)]

    def mlp_chunk(c):
        mlp_out[0] = mlp_out[0] + _mlp_chunk(c, h2, w1_ref, w2_ref)

    _ssd_block(n, z_ref, xbc_ref, sm_ref, cw_ref, cb_ref, dtb_ref, alog_ref, dskip_ref, nw_ref, shift_ref,
               cur_ref, ext_ref, xs_ref, y_ref, st_ref, after_group=mlp_chunk)
    xo = x + mlp_out[0] * valid
    if emit_stream:
        xo_ref[0] = xo
    hn_ref[0] = _rms(xo, gn_ref[...]).astype(hn_ref.dtype)


def _tail(proj, small, y_gla, y_swa, x, ssd_params, merge_params, mlp_params, final):
    bsz, L, d = x.shape
    nb = L // SSD_CHUNK
    assert LEAD == SSD_CHUNK
    cur = lambda seg: (lambda b, n: (b, jnp.minimum(n, nb - 1), seg))
    lag = lambda seg: (lambda b, n: (b, jnp.maximum(n - 1, 0), seg))
    if final:
        out_specs = [pl.BlockSpec((1, SSD_CHUNK, d), lambda b, n: (b, jnp.maximum(n - 2, 0), 0))]
        out_shape = [jax.ShapeDtypeStruct((bsz, L - LEAD, d), F32)]
    else:
        out_specs = [pl.BlockSpec((1, SSD_CHUNK, d), lag(0)), pl.BlockSpec((1, SSD_CHUNK, d), lag(0))]
        out_shape = [jax.ShapeDtypeStruct((bsz, L, d), F32), jax.ShapeDtypeStruct((bsz, L, d), BF16)]
    const = lambda b, n: (0, 0)
    single = pl.Buffered(1)
    conv_w, conv_b, dt_bias, a_log, d_exp, ssd_nw = ssd_params
    wa, wb, wc, wo, b_gate = merge_params
    norm_w, w1, w2, next_norm_w = mlp_params
    return pl.pallas_call(
        functools.partial(_tail_kernel, emit_stream=not final),
        grid=(bsz, nb + 1),
        in_specs=[pl.BlockSpec((1, SSD_CHUNK, SSD_DINNER), cur(_seg_block(SEG_Z))),
                  pl.BlockSpec((1, SSD_CHUNK, SSD_XBC), cur(_seg_block(SEG_XBC))),
                  pl.BlockSpec((1, SSD_CHUNK, N_SMALL), cur(0)),
                  pl.BlockSpec((SSD_CONV, SSD_XBC), const),
                  pl.BlockSpec((1, SSD_XBC), const),
                  pl.BlockSpec((1, N_SMALL), const),
                  pl.BlockSpec((1, N_SMALL), const),
                  pl.BlockSpec((1, SSD_DINNER), const),
                  pl.BlockSpec((1, SSD_DINNER), const),
                  pl.BlockSpec(((SSD_CONV - 1) * SSD_CHUNK, 2 * SSD_CHUNK), const),
                  pl.BlockSpec((1, SSD_CHUNK, GLA_DV), lag(0)),
                  pl.BlockSpec((1, SSD_CHUNK, SWA_HQ * SWA_HD), lag(0)),
                  pl.BlockSpec((1, SSD_CHUNK, N_BRANCH * D_MODEL), lag(_seg_block(SEG_GATE))),
                  pl.BlockSpec((1, SSD_CHUNK, d), lag(0)),
                  pl.BlockSpec((GLA_DV, d), const, pipeline_mode=single),
                  pl.BlockSpec((SWA_HQ * SWA_HD, d), const, pipeline_mode=single),
                  pl.BlockSpec((SSD_DINNER, d), const, pipeline_mode=single),
                  pl.BlockSpec((d, d), const, pipeline_mode=single),
                  pl.BlockSpec((1, N_BRANCH * D_MODEL), const),
                  pl.BlockSpec((1, d), const),
                  pl.BlockSpec((d, D_FF), const, pipeline_mode=single),
                  pl.BlockSpec((D_FF, d), const, pipeline_mode=single),
                  pl.BlockSpec((1, d), const)],
        out_specs=out_specs,
        out_shape=out_shape,
        scratch_shapes=[pltpu.VMEM((2 * SSD_CHUNK, SSD_XBC), BF16),
                        pltpu.VMEM((SSD_CHUNK, SSD_DINNER), F32),
                        pltpu.VMEM((SSD_CHUNK, SSD_DINNER), F32),
                        pltpu.VMEM((SSD_DINNER, SSD_DSTATE), F32),
                        pltpu.VMEM((SSD_CHUNK, SSD_DINNER), BF16),
                        pltpu.VMEM((SSD_CHUNK, SSD_DINNER), BF16)],
        compiler_params=pltpu.CompilerParams(
            dimension_semantics=("arbitrary", "arbitrary"), vmem_limit_bytes=VMEM_LIMIT),
        name="ssd_merge_mlp",
    )(proj, proj, small, conv_w, conv_b, dt_bias, a_log, d_exp, ssd_nw, _conv_shift_matrix(),
      y_gla, y_swa, proj, x, wa, wb, wc, wo, b_gate, norm_w, w1, w2, next_norm_w)


def _conv_shift_matrix():
    r = jnp.arange((SSD_CONV - 1) * SSD_CHUNK)
    source = SSD_CHUNK + r % SSD_CHUNK - (r // SSD_CHUNK + 1)
    return (jnp.arange(2 * SSD_CHUNK)[None, :] == source[:, None]).astype(BF16)


def _split_w_in(w_in):
    sizes = (GLA_DK, GLA_DK, GLA_DV, GLA_DV, GLA_RANK, SWA_HQ * SWA_HD, SWA_HKV * SWA_HD,
             SWA_HKV * SWA_HD, SSD_DINNER, SSD_XBC, SSD_HEADS, N_BRANCH * D_MODEL)
    starts = [sum(sizes[:i]) for i in range(len(sizes))]
    part = lambda i: w_in[:, :, starts[i]:starts[i] + sizes[i]]
    g_q, g_k, g_v, g_r, g_lr, s_q, s_k, s_v, m_z, m_xbc, m_dt, gate = (part(i) for i in range(len(sizes)))
    big = jnp.concatenate([gate, m_xbc, m_z, g_v, g_r, s_q, g_q, g_k, s_k, s_v], axis=-1).astype(BF16)
    pad = jnp.zeros(w_in.shape[:2] + (N_SMALL - SSD_HEADS - GLA_RANK,), w_in.dtype)
    small = jnp.concatenate([m_dt, g_lr, pad], axis=-1).astype(BF16)
    return big, small


def _pad_lanes(v, offset=0):
    out = jnp.zeros((v.shape[0], 1, LANES), F32)
    return out.at[:, 0, offset:offset + v.shape[1]].set(v.astype(F32))


def kernel(x, meta_tokens, norm_mix, w_in, b_gate, gla_w_gate_up, gla_b_gate_up, gla_norm, swa_sinks,
           ssd_conv_w, ssd_conv_b, ssd_dt_bias, ssd_A_log, ssd_D, ssd_norm,
           w_branch_gla, w_branch_swa, w_branch_ssd, w_out, norm_mlp, w_mlp_in, w_mlp_out, final_norm):
    bsz, seq, d = x.shape
    L = LEAD + seq

    w_big, w_small = _split_w_in(w_in)
    wg = jnp.zeros((DEPTH, N_SMALL, GLA_DK), F32).at[:, SMALL_LR:SMALL_LR + GLA_RANK, :].set(gla_w_gate_up)
    wg = wg.astype(BF16)
    dt_bias = _pad_lanes(ssd_dt_bias, SMALL_DT)
    a_log = _pad_lanes(ssd_A_log, SMALL_DT)
    d_exp = jnp.repeat(ssd_D.astype(F32), SSD_HEADDIM, axis=-1)[:, None, :]
    wa, wb, wc, wo = (w.astype(BF16) for w in (w_branch_gla, w_branch_swa, w_branch_ssd, w_out))
    w1, w2 = w_mlp_in.astype(BF16), w_mlp_out.astype(BF16)
    row = lambda p: p.astype(F32)[:, None, :]
    norm_mix_r, norm_mlp_r, b_gate_r = row(norm_mix), row(norm_mlp), row(b_gate)
    gla_bg_r, gla_nw_r = row(gla_b_gate_up), row(gla_norm)
    conv_b_r, ssd_nw_r = row(ssd_conv_b), row(ssd_norm)
    final_r = final_norm.astype(F32)[None, :]

    h, hn = _embed(x.astype(F32), meta_tokens.astype(F32), norm_mix_r[0])
    for i in range(DEPTH):
        proj, small = _inproj(hn.reshape(bsz * L, d), w_big[i], w_small[i])
        proj = proj.reshape(bsz, L, N_BIG)
        small = small.reshape(bsz, L, N_SMALL)
        y_gla, y_swa = _mix(proj, small, wg[i], gla_bg_r[i], gla_nw_r[i], swa_sinks[i].astype(F32))
        last = i == DEPTH - 1
        next_g = final_r if last else norm_mix_r[i + 1]
        outs = _tail(proj, small, y_gla, y_swa, h,
                     (ssd_conv_w[i].astype(F32), conv_b_r[i], dt_bias[i], a_log[i], d_exp[i], ssd_nw_r[i]),
                     (wa[i], wb[i], wc[i], wo[i], b_gate_r[i]),
                     (norm_mlp_r[i], w1[i], w2[i], next_g), final=last)
        if last:
            return outs[0]
        h, hn = outs
```

```python
import functools

import jax
import jax.numpy as jnp
from jax import lax
from jax.experimental import pallas as pl
from jax.experimental.pallas import tpu as pltpu

F32 = jnp.float32
BF16 = jnp.bfloat16

D_MODEL = 1024
DEPTH = 4
N_META = 16
LEAD = 128
PAD = LEAD - N_META
EPS = 1e-6

GLA_HEADS = 4
GLA_DK = D_MODEL // 2
GLA_DV = D_MODEL
GLA_HK = GLA_DK // GLA_HEADS
GLA_HV = GLA_DV // GLA_HEADS
GLA_RANK = 16
GLA_NORMALIZER = 16.0
GLA_CHUNK = 64

SWA_HQ = 16
SWA_HKV = 2
SWA_HD = 64
SWA_BLOCK = 128

SSD_DINNER = 2 * D_MODEL
SSD_HEADDIM = 64
SSD_HEADS = SSD_DINNER // SSD_HEADDIM
SSD_GROUPS = 4
SSD_DSTATE = 128
SSD_CONV = 4
SSD_CHUNK = 128
SSD_BC = SSD_GROUPS * SSD_DSTATE
SSD_XBC = SSD_DINNER + 2 * SSD_BC
SSD_GROUP_WIDTH = SSD_DINNER // SSD_GROUPS

N_BRANCH = 3
D_FF = 4 * D_MODEL

LANES = 128
SUBLANES = 8

SEG_GATE, SEG_XBC, SEG_Z, SEG_GV, SEG_GR, SEG_SQ, SEG_GQ, SEG_GK, SEG_SK, SEG_SV = range(10)
_SEG_WIDTHS = (N_BRANCH * D_MODEL, SSD_XBC, SSD_DINNER, GLA_DV, GLA_DV,
               SWA_HQ * SWA_HD, GLA_DK, GLA_DK, SWA_HKV * SWA_HD, SWA_HKV * SWA_HD)
_SEG_STARTS = tuple(sum(_SEG_WIDTHS[:i]) for i in range(len(_SEG_WIDTHS)))
N_BIG = sum(_SEG_WIDTHS)
assert all(s % w == 0 for s, w in zip(_SEG_STARTS, _SEG_WIDTHS))
SMALL_DT = 0
SMALL_LR = SSD_HEADS
N_SMALL = LANES

PROJ_TM = 3072
PROJ_TN = 1792
PROJ_SUB = 256
GLA_ROWS = 128
VMEM_LIMIT = 56 * 1024 * 1024


def _seg_block(seg):
    return _SEG_STARTS[seg] // _SEG_WIDTHS[seg]


def _nt(a, b):
    return lax.dot_general(a, b, (((1,), (1,)), ((), ())), preferred_element_type=F32)


def _tn(a, b):
    return lax.dot_general(a, b, (((0,), (0,)), ((), ())), preferred_element_type=F32)


def _split_dot(m01, x):
    hi = x.astype(BF16)
    r1 = x - hi.astype(F32)
    mid = r1.astype(BF16)
    lo = (r1 - mid.astype(F32)).astype(BF16)
    return (jnp.dot(m01, hi, preferred_element_type=F32)
            + jnp.dot(m01, mid, preferred_element_type=F32)
            + jnp.dot(m01, lo, preferred_element_type=F32))


def _tril_ones(n):
    r = lax.broadcasted_iota(jnp.int32, (n, n), 0)
    c = lax.broadcasted_iota(jnp.int32, (n, n), 1)
    return r >= c


def _log1p_exp_neg_abs(x):
    u = jnp.exp(-jnp.abs(x))
    w = 1.0 + u
    d = w - 1.0
    return jnp.where(d == 0.0, u, jnp.log(w) * (u / d))


def _softplus(x):
    return jnp.maximum(x, 0.0) + _log1p_exp_neg_abs(x)


def _log_sigmoid(x):
    return jnp.minimum(x, 0.0) - _log1p_exp_neg_abs(x)


def _silu(x):
    return x * jax.nn.sigmoid(x)


def _rms(x, g):
    return x * lax.rsqrt(jnp.mean(x * x, axis=-1, keepdims=True) + EPS) * g


def _embed_kernel(x_ref, meta_ref, g_ref, h_ref, hn_ref):
    j = pl.program_id(1)

    def emit(block):
        h_ref[0] = block
        hn_ref[0] = _rms(block, g_ref[...]).astype(hn_ref.dtype)

    @pl.when(j == 0)
    def _():
        emit(jnp.concatenate([jnp.zeros((PAD, D_MODEL), F32), meta_ref[...]], axis=0))

    @pl.when(j > 0)
    def _():
        emit(x_ref[0])


def _embed(x, meta_tokens, g):
    bsz, seq, d = x.shape
    L = LEAD + seq
    row = lambda b, j: (b, j, 0)
    return pl.pallas_call(
        _embed_kernel,
        grid=(bsz, L // LEAD),
        in_specs=[pl.BlockSpec((1, LEAD, d), lambda b, j: (b, jnp.maximum(j - 1, 0), 0)),
                  pl.BlockSpec((N_META, d), lambda b, j: (0, 0)),
                  pl.BlockSpec((1, d), lambda b, j: (0, 0))],
        out_specs=[pl.BlockSpec((1, LEAD, d), row), pl.BlockSpec((1, LEAD, d), row)],
        out_shape=[jax.ShapeDtypeStruct((bsz, L, d), F32), jax.ShapeDtypeStruct((bsz, L, d), BF16)],
        name="embed_norm",
    )(x, meta_tokens, g)


def _inproj_kernel(h_ref, w_ref, ws_ref, o_ref, os_ref):
    for c in range(PROJ_TN // PROJ_SUB):
        cols = slice(c * PROJ_SUB, (c + 1) * PROJ_SUB)
        o_ref[:, cols] = jnp.dot(h_ref[...], w_ref[:, cols], preferred_element_type=F32).astype(o_ref.dtype)

    @pl.when(pl.program_id(1) == 0)
    def _():
        os_ref[...] = jnp.dot(h_ref[...], ws_ref[...], preferred_element_type=F32)


def _inproj(hn, w_big, w_small):
    T, d = hn.shape
    return pl.pallas_call(
        _inproj_kernel,
        grid=(T // PROJ_TM, N_BIG // PROJ_TN),
        in_specs=[pl.BlockSpec((PROJ_TM, d), lambda i, j: (i, 0)),
                  pl.BlockSpec((d, PROJ_TN), lambda i, j: (0, j)),
                  pl.BlockSpec((d, N_SMALL), lambda i, j: (0, 0))],
        out_specs=[pl.BlockSpec((PROJ_TM, PROJ_TN), lambda i, j: (i, j)),
                   pl.BlockSpec((PROJ_TM, N_SMALL), lambda i, j: (i, 0))],
        out_shape=[jax.ShapeDtypeStruct((T, N_BIG), BF16),
                   jax.ShapeDtypeStruct((T, N_SMALL), F32)],
        compiler_params=pltpu.CompilerParams(
            dimension_semantics=("arbitrary", "arbitrary"), vmem_limit_bytes=VMEM_LIMIT),
        name="in_proj",
    )(hn, w_big, w_small)


def _gla_units(q_ref, k_ref, v_ref, r_ref, sm_ref, wg_ref, bg_ref, nw_ref, o_ref, st_ref):
    logits = jnp.dot(sm_ref[0].astype(BF16), wg_ref[...], preferred_element_type=F32) + bg_ref[...]
    gk_all = _log_sigmoid(logits) * (1.0 / GLA_NORMALIZER)
    causal = _tril_ones(GLA_CHUNK)
    tril = causal.astype(BF16)

    def unit(rows, g_all, h):
        hk = slice(h * GLA_HK, (h + 1) * GLA_HK)
        hv = slice(h * GLA_HV, (h + 1) * GLA_HV)
        g = g_all[:, hk]
        g_last = g[GLA_CHUNK - 1:GLA_CHUNK, :]
        q = q_ref[0, rows, hk].astype(F32) * (GLA_HK ** -0.5)
        k = k_ref[0, rows, hk].astype(F32)
        v = v_ref[0, rows, hv]
        q_dec = (q * jnp.exp(g)).astype(BF16)
        k_inv = (k * jnp.exp(-g)).astype(BF16)
        k_end = (k * jnp.exp(g_last - g)).astype(BF16)
        attn = jnp.where(causal, _nt(q_dec, k_inv), 0.0).astype(BF16)
        state_t = st_ref[h]
        o = jnp.dot(attn, v, preferred_element_type=F32) + _nt(q_dec, state_t.astype(BF16))
        st_ref[h] = state_t * jnp.exp(g_last) + _tn(v, k_end)
        o = o * lax.rsqrt(jnp.mean(o * o, axis=-1, keepdims=True) + EPS) * nw_ref[...]
        o = o * _silu(r_ref[0, rows, hv].astype(F32))
        o_ref[0, rows, hv] = o.astype(o_ref.dtype)

    units = []
    for c in range(GLA_ROWS // GLA_CHUNK):
        rows = slice(c * GLA_CHUNK, (c + 1) * GLA_CHUNK)
        g_all = _split_dot(tril, gk_all[rows])
        units += [functools.partial(unit, rows, g_all, h) for h in range(GLA_HEADS)]
    return units


def _swa_units(n, q_ref, kp_ref, kc_ref, vp_ref, vc_ref, sink_ref, o_ref):
    nk = 2 * SWA_BLOCK
    low_q = lax.broadcasted_iota(jnp.int32, (SWA_BLOCK, LANES), 1) < SWA_HD
    low_k = lax.broadcasted_iota(jnp.int32, (nk, LANES), 1) < SWA_HD

    both_k = jnp.concatenate([kp_ref[0], kc_ref[0]], axis=0).astype(F32)
    swapped_k = pltpu.roll(both_k, SWA_HD, axis=1)
    k_ext = (jnp.where(low_k, both_k, swapped_k).astype(BF16), jnp.where(low_k, swapped_k, both_k).astype(BF16))
    both_v = jnp.concatenate([vp_ref[0], vc_ref[0]], axis=0).astype(F32)
    swapped_v = pltpu.roll(both_v, SWA_HD, axis=1)
    v_ones = []
    for in_low, in_high in ((both_v, swapped_v), (swapped_v, both_v)):
        v_ones.append((jnp.where(low_k, in_low, 1.0).astype(BF16),
                       jnp.where(low_k, 1.0, in_high).astype(BF16)))
    qi = lax.broadcasted_iota(jnp.int32, (SWA_BLOCK, nk), 0)
    kk = lax.broadcasted_iota(jnp.int32, (SWA_BLOCK, nk), 1)
    allowed = (kk > qi) & (kk <= qi + SWA_BLOCK) & (kk >= PAD + SWA_BLOCK - SWA_BLOCK * n)
    heads_per_kv = SWA_HQ // SWA_HKV

    def unit(j):
        hkv = (2 * j) // heads_per_kv
        cols = slice(j * LANES, (j + 1) * LANES)
        q_pair = q_ref[0, :, cols] * (SWA_HD ** -0.5)
        full, sink_term = [], []
        for half in range(2):
            q_half = jnp.where(low_q if half == 0 else ~low_q, q_pair, jnp.zeros_like(q_pair))
            s = jnp.where(allowed, _nt(q_half, k_ext[hkv]), -jnp.inf)
            sink = sink_ref[2 * j + half]
            m = jnp.maximum(jnp.max(s, axis=-1, keepdims=True), sink)
            p = jnp.exp(s - m).astype(BF16)
            full.append(jnp.dot(p, v_ones[hkv][half], preferred_element_type=F32))
            sink_term.append(jnp.exp(sink - m))
        num = jnp.where(low_q, full[0], full[1])
        sums = pltpu.roll(jnp.where(low_q, full[1], full[0]), SWA_HD, axis=1)
        den = sums + jnp.where(low_q, sink_term[0], sink_term[1])
        o_ref[0, :, cols] = (num * (1.0 / den)).astype(o_ref.dtype)

    return [functools.partial(unit, j) for j in range(SWA_HQ // 2)]


def _mix_kernel(gq_ref, gk_ref, gv_ref, gr_ref, sm_ref, wg_ref, bg_ref, nw_ref,
                sq_ref, kp_ref, kc_ref, vp_ref, vc_ref, sink_ref,
                gla_ref, swa_ref, st_ref):
    n = pl.program_id(1)

    @pl.when(n == 0)
    def _():
        st_ref[...] = jnp.zeros_like(st_ref)

    gla = _gla_units(gq_ref, gk_ref, gv_ref, gr_ref, sm_ref, wg_ref, bg_ref, nw_ref, gla_ref, st_ref)
    swa = _swa_units(n, sq_ref, kp_ref, kc_ref, vp_ref, vc_ref, sink_ref, swa_ref)
    for unit in gla + swa:
        unit()


def _mix(proj, small, wg, bg, nw, sinks):
    bsz, L, _ = proj.shape
    assert GLA_ROWS == SWA_BLOCK
    kvw = SWA_HKV * SWA_HD
    cur = lambda seg: (lambda b, n: (b, n, _seg_block(seg)))
    prev = lambda seg: (lambda b, n: (b, jnp.maximum(n - 1, 0), _seg_block(seg)))
    const = lambda b, n: (0, 0)
    rows = lambda b, n: (b, n, 0)
    return pl.pallas_call(
        _mix_kernel,
        grid=(bsz, L // GLA_ROWS),
        in_specs=[pl.BlockSpec((1, GLA_ROWS, GLA_DK), cur(SEG_GQ)),
                  pl.BlockSpec((1, GLA_ROWS, GLA_DK), cur(SEG_GK)),
                  pl.BlockSpec((1, GLA_ROWS, GLA_DV), cur(SEG_GV)),
                  pl.BlockSpec((1, GLA_ROWS, GLA_DV), cur(SEG_GR)),
                  pl.BlockSpec((1, GLA_ROWS, N_SMALL), rows),
                  pl.BlockSpec((N_SMALL, GLA_DK), const),
                  pl.BlockSpec((1, GLA_DK), const),
                  pl.BlockSpec((1, GLA_HV), const),
                  pl.BlockSpec((1, SWA_BLOCK, SWA_HQ * SWA_HD), cur(SEG_SQ)),
                  pl.BlockSpec((1, SWA_BLOCK, kvw), prev(SEG_SK)),
                  pl.BlockSpec((1, SWA_BLOCK, kvw), cur(SEG_SK)),
                  pl.BlockSpec((1, SWA_BLOCK, kvw), prev(SEG_SV)),
                  pl.BlockSpec((1, SWA_BLOCK, kvw), cur(SEG_SV)),
                  pl.BlockSpec(memory_space=pltpu.SMEM)],
        out_specs=[pl.BlockSpec((1, GLA_ROWS, GLA_DV), rows),
                   pl.BlockSpec((1, SWA_BLOCK, SWA_HQ * SWA_HD), rows)],
        out_shape=[jax.ShapeDtypeStruct((bsz, L, GLA_DV), BF16),
                   jax.ShapeDtypeStruct((bsz, L, SWA_HQ * SWA_HD), BF16)],
        scratch_shapes=[pltpu.VMEM((GLA_HEADS, GLA_HV, GLA_HK), F32)],
        compiler_params=pltpu.CompilerParams(
            dimension_semantics=("arbitrary", "arbitrary"), vmem_limit_bytes=VMEM_LIMIT),
        name="gla_swa",
    )(proj, proj, proj, proj, small, wg, bg, nw, proj, proj, proj, proj, proj, sinks)


def _ssd_block(n, z_ref, xbc_ref, sm_ref, cw_ref, cb_ref, dtb_ref, alog_ref, dskip_ref, nw_ref,
               o_ref, ext_ref, xs_ref, y_ref, st_ref, after_group):
    rows = SSD_CHUNK

    col_chunk = 512
    bc_parts = []
    for c in range(SSD_XBC // col_chunk):
        cols = slice(c * col_chunk, (c + 1) * col_chunk)
        x_cur = xbc_ref[0, :, cols].astype(F32)
        x_ext = jnp.concatenate([ext_ref[:, cols], x_cur], axis=0)
        acc = cb_ref[:, cols] + cw_ref[SSD_CONV - 1:SSD_CONV, cols] * x_cur
        for back in range(1, SSD_CONV):
            tap = SSD_CONV - 1 - back
            shifted = pltpu.roll(x_ext, back, axis=0)[SUBLANES:, :]
            acc = acc + cw_ref[tap:tap + 1, cols] * shifted
        ext_ref[:, cols] = x_cur[rows - SUBLANES:, :]
        act = _silu(acc)
        if c < SSD_DINNER // col_chunk:
            xs_ref[:, cols] = act
        else:
            bc_parts.append(act.astype(BF16))
    b_all, c_all = bc_parts

    lane = lax.broadcasted_iota(jnp.int32, (1, LANES), 1)
    pos = n * rows + lax.broadcasted_iota(jnp.int32, (rows, 1), 0)
    valid = (pos >= PAD).astype(F32)
    dt = _softplus(sm_ref[0] + dtb_ref[...]) * valid
    a_coef = jnp.where(lane < SSD_HEADS, -jnp.exp(alog_ref[...]), 0.0)
    causal = _tril_ones(rows)
    a_cs = _split_dot(causal.astype(BF16), dt * a_coef)
    a_cs_t = a_cs.T
    a_last_t = a_cs_t[:, rows - 1:rows]
    dt_t = dt.T
    w_end_t = dt_t * jnp.exp(a_last_t - a_cs_t)
    c_decay_t = jnp.exp(a_last_t)

    low = lax.broadcasted_iota(jnp.int32, (rows, LANES), 1) < SSD_HEADDIM
    top = lax.broadcasted_iota(jnp.int32, (LANES, 1), 0) < SSD_HEADDIM

    def col(x, h):
        return jnp.broadcast_to(x[:, h:h + 1], (rows, LANES))

    pairs_per_group = SSD_GROUP_WIDTH // LANES
    for g in range(SSD_GROUPS):
        gs = slice(g * SSD_DSTATE, (g + 1) * SSD_DSTATE)
        b_g = b_all[:, gs]
        c_g = c_all[:, gs]
        cb = _nt(c_g, b_g)
        srows = slice(g * SSD_GROUP_WIDTH, (g + 1) * SSD_GROUP_WIDTH)
        y_off = _nt(c_g, st_ref[srows, :].astype(BF16))
        ssq = jnp.zeros((rows, 1), F32)
        x_end_t, decay = [], []
        for jj in range(pairs_per_group):
            j = g * pairs_per_group + jj
            ha, hb = 2 * j, 2 * j + 1
            cols = slice(j * LANES, (j + 1) * LANES)
            xs = xs_ref[:, cols]
            xs_bf = xs.astype(BF16)
            a_col = (col(a_cs, ha), col(a_cs, hb))
            m_pair = []
            for h, a_c in ((ha, a_col[0]), (hb, a_col[1])):
                seg = a_c - a_cs_t[h:h + 1, :]
                m_h = cb * jnp.exp(jnp.where(causal, seg, -jnp.inf)) * dt_t[h:h + 1, :]
                m_pair.append(m_h.astype(BF16))
            zero = jnp.zeros_like(xs_bf)
            xs_split = jnp.concatenate([jnp.where(low, xs_bf, zero), jnp.where(low, zero, xs_bf)], axis=0)
            y = jnp.dot(jnp.concatenate(m_pair, axis=1), xs_split, preferred_element_type=F32)
            y = y + y_off[:, jj * LANES:(jj + 1) * LANES] * jnp.exp(jnp.where(low, a_col[0], a_col[1]))
            y = y + dskip_ref[:, cols] * xs
            w_rows = jnp.where(top, w_end_t[ha:ha + 1, :], w_end_t[hb:hb + 1, :])
            x_end_t.append((xs.T * w_rows).astype(BF16))
            decay.append(jnp.where(top, c_decay_t[ha:ha + 1, :], c_decay_t[hb:hb + 1, :]))
            y = y * _silu(z_ref[0, :, cols].astype(F32))
            ssq = ssq + jnp.sum(y * y, axis=-1, keepdims=True)
            y_ref[:, cols] = y
        st_ref[srows, :] = (st_ref[srows, :] * jnp.concatenate(decay, axis=0)
                            + jnp.dot(jnp.concatenate(x_end_t, axis=0), b_g, preferred_element_type=F32))
        scale = lax.rsqrt(ssq * (1.0 / SSD_GROUP_WIDTH) + EPS)
        gcols = slice(g * SSD_GROUP_WIDTH, (g + 1) * SSD_GROUP_WIDTH)
        o_ref[:, gcols] = (y_ref[:, gcols] * scale * nw_ref[:, gcols]).astype(o_ref.dtype)
        after_group(g)


def _merge_block(j, ga_ref, sw_ref, sd_ref, gl_ref, x_ref, wa_ref, wb_ref, wc_ref, wo_ref, bg_ref):
    rows = SSD_CHUNK
    pos = j * rows + lax.broadcasted_iota(jnp.int32, (rows, 1), 0)
    valid = (pos >= PAD).astype(F32)
    gates = jax.nn.sigmoid(gl_ref[0].astype(F32) + bg_ref[...])
    merged = gates[:, 0:D_MODEL] * jnp.dot(ga_ref[0], wa_ref[...], preferred_element_type=F32)
    merged += gates[:, D_MODEL:2 * D_MODEL] * jnp.dot(sw_ref[0], wb_ref[...], preferred_element_type=F32)
    merged += gates[:, 2 * D_MODEL:] * jnp.dot(sd_ref[...], wc_ref[...], preferred_element_type=F32)
    x = x_ref[0] + jnp.dot(merged.astype(BF16), wo_ref[...], preferred_element_type=F32) * valid
    return x, valid


def _mlp_chunk(c, h2, w1_ref, w2_ref):
    ff_chunk = D_FF // SSD_GROUPS
    cols = slice(c * ff_chunk, (c + 1) * ff_chunk)
    hid = jnp.dot(h2, w1_ref[:, cols], preferred_element_type=F32)
    hid = jnp.square(jnp.maximum(hid, 0.0)).astype(BF16)
    return jnp.dot(hid, w2_ref[cols, :], preferred_element_type=F32)


def _tail_kernel(z_ref, xbc_ref, sm_ref, cw_ref, cb_ref, dtb_ref, alog_ref, dskip_ref, nw_ref,
                 ga_ref, sw_ref, gl_ref, x_ref, wa_ref, wb_ref, wc_ref, wo_ref, bg_ref,
                 nm_ref, w1_ref, w2_ref, gn_ref, *outs_and_scratch, emit_stream):
    if emit_stream:
        xo_ref, hn_ref, ext_ref, xs_ref, y_ref, st_ref, cur_ref, prev_ref = outs_and_scratch
    else:
        hn_ref, ext_ref, xs_ref, y_ref, st_ref, cur_ref, prev_ref = outs_and_scratch
    n = pl.program_id(1)

    @pl.when(n == 0)
    def _():
        ext_ref[...] = jnp.zeros_like(ext_ref)
        st_ref[...] = jnp.zeros_like(st_ref)
        cur_ref[...] = jnp.zeros_like(cur_ref)

    prev_ref[...] = cur_ref[...]
    x, valid = _merge_block(n - 1, ga_ref, sw_ref, prev_ref, gl_ref, x_ref, wa_ref, wb_ref, wc_ref, wo_ref,
                            bg_ref)
    h2 = _rms(x, nm_ref[...]).astype(BF16)
    mlp_out = [jnp.zeros_like(x)]

    def mlp_chunk(c):
        mlp_out[0] = mlp_out[0] + _mlp_chunk(c, h2, w1_ref, w2_ref)

    _ssd_block(n, z_ref, xbc_ref, sm_ref, cw_ref, cb_ref, dtb_ref, alog_ref, dskip_ref, nw_ref,
               cur_ref, ext_ref, xs_ref, y_ref, st_ref, after_group=mlp_chunk)
    xo = x + mlp_out[0] * valid
    if emit_stream:
        xo_ref[0] = xo
    hn_ref[0] = _rms(xo, gn_ref[...]).astype(hn_ref.dtype)


def _tail(proj, small, y_gla, y_swa, x, ssd_params, merge_params, mlp_params, final):
    bsz, L, d = x.shape
    nb = L // SSD_CHUNK
    assert LEAD == SSD_CHUNK
    cur = lambda seg: (lambda b, n: (b, jnp.minimum(n, nb - 1), seg))
    lag = lambda seg: (lambda b, n: (b, jnp.maximum(n - 1, 0), seg))
    if final:
        out_specs = [pl.BlockSpec((1, SSD_CHUNK, d), lambda b, n: (b, jnp.maximum(n - 2, 0), 0))]
        out_shape = [jax.ShapeDtypeStruct((bsz, L - LEAD, d), F32)]
    else:
        out_specs = [pl.BlockSpec((1, SSD_CHUNK, d), lag(0)), pl.BlockSpec((1, SSD_CHUNK, d), lag(0))]
        out_shape = [jax.ShapeDtypeStruct((bsz, L, d), F32), jax.ShapeDtypeStruct((bsz, L, d), BF16)]
    const = lambda b, n: (0, 0)
    single = pl.Buffered(1)
    conv_w, conv_b, dt_bias, a_log, d_exp, ssd_nw = ssd_params
    wa, wb, wc, wo, b_gate = merge_params
    norm_w, w1, w2, next_norm_w = mlp_params
    return pl.pallas_call(
        functools.partial(_tail_kernel, emit_stream=not final),
        grid=(bsz, nb + 1),
        in_specs=[pl.BlockSpec((1, SSD_CHUNK, SSD_DINNER), cur(_seg_block(SEG_Z))),
                  pl.BlockSpec((1, SSD_CHUNK, SSD_XBC), cur(_seg_block(SEG_XBC))),
                  pl.BlockSpec((1, SSD_CHUNK, N_SMALL), cur(0)),
                  pl.BlockSpec((SSD_CONV, SSD_XBC), const),
                  pl.BlockSpec((1, SSD_XBC), const),
                  pl.BlockSpec((1, N_SMALL), const),
                  pl.BlockSpec((1, N_SMALL), const),
                  pl.BlockSpec((1, SSD_DINNER), const),
                  pl.BlockSpec((1, SSD_DINNER), const),
                  pl.BlockSpec((1, SSD_CHUNK, GLA_DV), lag(0)),
                  pl.BlockSpec((1, SSD_CHUNK, SWA_HQ * SWA_HD), lag(0)),
                  pl.BlockSpec((1, SSD_CHUNK, N_BRANCH * D_MODEL), lag(_seg_block(SEG_GATE))),
                  pl.BlockSpec((1, SSD_CHUNK, d), lag(0)),
                  pl.BlockSpec((GLA_DV, d), const, pipeline_mode=single),
                  pl.BlockSpec((SWA_HQ * SWA_HD, d), const, pipeline_mode=single),
                  pl.BlockSpec((SSD_DINNER, d), const, pipeline_mode=single),
                  pl.BlockSpec((d, d), const, pipeline_mode=single),
                  pl.BlockSpec((1, N_BRANCH * D_MODEL), const),
                  pl.BlockSpec((1, d), const),
                  pl.BlockSpec((d, D_FF), const, pipeline_mode=single),
                  pl.BlockSpec((D_FF, d), const, pipeline_mode=single),
                  pl.BlockSpec((1, d), const)],
        out_specs=out_specs,
        out_shape=out_shape,
        scratch_shapes=[pltpu.VMEM((SUBLANES, SSD_XBC), F32),
                        pltpu.VMEM((SSD_CHUNK, SSD_DINNER), F32),
                        pltpu.VMEM((SSD_CHUNK, SSD_DINNER), F32),
                        pltpu.VMEM((SSD_DINNER, SSD_DSTATE), F32),
                        pltpu.VMEM((SSD_CHUNK, SSD_DINNER), BF16),
                        pltpu.VMEM((SSD_CHUNK, SSD_DINNER), BF16)],
        compiler_params=pltpu.CompilerParams(
            dimension_semantics=("arbitrary", "arbitrary"), vmem_limit_bytes=VMEM_LIMIT),
        name="ssd_merge_mlp",
    )(proj, proj, small, conv_w, conv_b, dt_bias, a_log, d_exp, ssd_nw,
      y_gla, y_swa, proj, x, wa, wb, wc, wo, b_gate, norm_w, w1, w2, next_norm_w)


def _split_w_in(w_in):
    sizes = (GLA_DK, GLA_DK, GLA_DV, GLA_DV, GLA_RANK, SWA_HQ * SWA_HD, SWA_HKV * SWA_HD,
             SWA_HKV * SWA_HD, SSD_DINNER, SSD_XBC, SSD_HEADS, N_BRANCH * D_MODEL)
    starts = [sum(sizes[:i]) for i in range(len(sizes))]
    part = lambda i: w_in[:, :, starts[i]:starts[i] + sizes[i]]
    g_q, g_k, g_v, g_r, g_lr, s_q, s_k, s_v, m_z, m_xbc, m_dt, gate = (part(i) for i in range(len(sizes)))
    big = jnp.concatenate([gate, m_xbc, m_z, g_v, g_r, s_q, g_q, g_k, s_k, s_v], axis=-1).astype(BF16)
    pad = jnp.zeros(w_in.shape[:2] + (N_SMALL - SSD_HEADS - GLA_RANK,), w_in.dtype)
    small = jnp.concatenate([m_dt, g_lr, pad], axis=-1).astype(BF16)
    return big, small


def _pad_lanes(v, offset=0):
    out = jnp.zeros((v.shape[0], 1, LANES), F32)
    return out.at[:, 0, offset:offset + v.shape[1]].set(v.astype(F32))


def kernel(x, meta_tokens, norm_mix, w_in, b_gate, gla_w_gate_up, gla_b_gate_up, gla_norm, swa_sinks,
           ssd_conv_w, ssd_conv_b, ssd_dt_bias, ssd_A_log, ssd_D, ssd_norm,
           w_branch_gla, w_branch_swa, w_branch_ssd, w_out, norm_mlp, w_mlp_in, w_mlp_out, final_norm):
    bsz, seq, d = x.shape
    L = LEAD + seq

    w_big, w_small = _split_w_in(w_in)
    wg = jnp.zeros((DEPTH, N_SMALL, GLA_DK), F32).at[:, SMALL_LR:SMALL_LR + GLA_RANK, :].set(gla_w_gate_up)
    wg = wg.astype(BF16)
    dt_bias = _pad_lanes(ssd_dt_bias, SMALL_DT)
    a_log = _pad_lanes(ssd_A_log, SMALL_DT)
    d_exp = jnp.repeat(ssd_D.astype(F32), SSD_HEADDIM, axis=-1)[:, None, :]
    wa, wb, wc, wo = (w.astype(BF16) for w in (w_branch_gla, w_branch_swa, w_branch_ssd, w_out))
    w1, w2 = w_mlp_in.astype(BF16), w_mlp_out.astype(BF16)
    row = lambda p: p.astype(F32)[:, None, :]
    norm_mix_r, norm_mlp_r, b_gate_r = row(norm_mix), row(norm_mlp), row(b_gate)
    gla_bg_r, gla_nw_r = row(gla_b_gate_up), row(gla_norm)
    conv_b_r, ssd_nw_r = row(ssd_conv_b), row(ssd_norm)
    final_r = final_norm.astype(F32)[None, :]

    h, hn = _embed(x.astype(F32), meta_tokens.astype(F32), norm_mix_r[0])
    for i in range(DEPTH):
        proj, small = _inproj(hn.reshape(bsz * L, d), w_big[i], w_small[i])
        proj = proj.reshape(bsz, L, N_BIG)
        small = small.reshape(bsz, L, N_SMALL)
        y_gla, y_swa = _mix(proj, small, wg[i], gla_bg_r[i], gla_nw_r[i], swa_sinks[i].astype(F32))
        last = i == DEPTH - 1
        next_g = final_r if last else norm_mix_r[i + 1]
        outs = _tail(proj, small, y_gla, y_swa, h,
                     (ssd_conv_w[i].astype(F32), conv_b_r[i], dt_bias[i], a_log[i], d_exp[i], ssd_nw_r[i]),
                     (wa[i], wb[i], wc[i], wo[i], b_gate_r[i]),
                     (norm_mlp_r[i], w1[i], w2[i], next_g), final=last)
        if last:
            return outs[0]
        h, hn = outs
```

```python
import functools

import jax
import jax.numpy as jnp
from jax import lax
from jax.experimental import pallas as pl
from jax.experimental.pallas import tpu as pltpu

F32 = jnp.float32
BF16 = jnp.bfloat16

D_MODEL = 1024
DEPTH = 4
N_META = 16
LEAD = 128
PAD = LEAD - N_META
EPS = 1e-6

GLA_HEADS = 4
GLA_DK = D_MODEL // 2
GLA_DV = D_MODEL
GLA_HK = GLA_DK // GLA_HEADS
GLA_HV = GLA_DV // GLA_HEADS
GLA_RANK = 16
GLA_NORMALIZER = 16.0
GLA_CHUNK = 64

SWA_HQ = 16
SWA_HKV = 2
SWA_HD = 64
SWA_BLOCK = 128

SSD_DINNER = 2 * D_MODEL
SSD_HEADDIM = 64
SSD_HEADS = SSD_DINNER // SSD_HEADDIM
SSD_GROUPS = 4
SSD_DSTATE = 128
SSD_CONV = 4
SSD_CHUNK = 128
SSD_BC = SSD_GROUPS * SSD_DSTATE
SSD_XBC = SSD_DINNER + 2 * SSD_BC
SSD_GROUP_WIDTH = SSD_DINNER // SSD_GROUPS

N_BRANCH = 3
D_FF = 4 * D_MODEL

LANES = 128
SUBLANES = 8

SEG_GATE, SEG_XBC, SEG_Z, SEG_GV, SEG_GR, SEG_SQ, SEG_GQ, SEG_GK, SEG_SK, SEG_SV = range(10)
_SEG_WIDTHS = (N_BRANCH * D_MODEL, SSD_XBC, SSD_DINNER, GLA_DV, GLA_DV,
               SWA_HQ * SWA_HD, GLA_DK, GLA_DK, SWA_HKV * SWA_HD, SWA_HKV * SWA_HD)
_SEG_STARTS = tuple(sum(_SEG_WIDTHS[:i]) for i in range(len(_SEG_WIDTHS)))
N_BIG = sum(_SEG_WIDTHS)
assert all(s % w == 0 for s, w in zip(_SEG_STARTS, _SEG_WIDTHS))
SMALL_DT = 0
SMALL_LR = SSD_HEADS
N_SMALL = LANES

PROJ_TM = 3072
PROJ_TN = 1792
PROJ_SUB = 256
GLA_ROWS = 128
VMEM_LIMIT = 56 * 1024 * 1024


def _seg_block(seg):
    return _SEG_STARTS[seg] // _SEG_WIDTHS[seg]


def _nt(a, b):
    return lax.dot_general(a, b, (((1,), (1,)), ((), ())), preferred_element_type=F32)


def _tn(a, b):
    return lax.dot_general(a, b, (((0,), (0,)), ((), ())), preferred_element_type=F32)


def _split_dot(m01, x):
    hi = x.astype(BF16)
    r1 = x - hi.astype(F32)
    mid = r1.astype(BF16)
    lo = (r1 - mid.astype(F32)).astype(BF16)
    return (jnp.dot(m01, hi, preferred_element_type=F32)
            + jnp.dot(m01, mid, preferred_element_type=F32)
            + jnp.dot(m01, lo, preferred_element_type=F32))


def _tril_ones(n):
    r = lax.broadcasted_iota(jnp.int32, (n, n), 0)
    c = lax.broadcasted_iota(jnp.int32, (n, n), 1)
    return r >= c


def _log1p_exp_neg_abs(x):
    u = jnp.exp(-jnp.abs(x))
    w = 1.0 + u
    d = w - 1.0
    return jnp.where(d == 0.0, u, jnp.log(w) * (u / d))


def _softplus(x):
    return jnp.maximum(x, 0.0) + _log1p_exp_neg_abs(x)


def _log_sigmoid(x):
    return jnp.minimum(x, 0.0) - _log1p_exp_neg_abs(x)


def _silu(x):
    return x * jax.nn.sigmoid(x)


def _rms(x, g):
    return x * lax.rsqrt(jnp.mean(x * x, axis=-1, keepdims=True) + EPS) * g


def _embed_kernel(x_ref, meta_ref, g_ref, h_ref, hn_ref):
    j = pl.program_id(1)

    def emit(block):
        h_ref[0] = block
        hn_ref[0] = _rms(block, g_ref[...]).astype(hn_ref.dtype)

    @pl.when(j == 0)
    def _():
        emit(jnp.concatenate([jnp.zeros((PAD, D_MODEL), F32), meta_ref[...]], axis=0))

    @pl.when(j > 0)
    def _():
        emit(x_ref[0])


def _embed(x, meta_tokens, g):
    bsz, seq, d = x.shape
    L = LEAD + seq
    row = lambda b, j: (b, j, 0)
    return pl.pallas_call(
        _embed_kernel,
        grid=(bsz, L // LEAD),
        in_specs=[pl.BlockSpec((1, LEAD, d), lambda b, j: (b, jnp.maximum(j - 1, 0), 0)),
                  pl.BlockSpec((N_META, d), lambda b, j: (0, 0)),
                  pl.BlockSpec((1, d), lambda b, j: (0, 0))],
        out_specs=[pl.BlockSpec((1, LEAD, d), row), pl.BlockSpec((1, LEAD, d), row)],
        out_shape=[jax.ShapeDtypeStruct((bsz, L, d), F32), jax.ShapeDtypeStruct((bsz, L, d), BF16)],
        name="embed_norm",
    )(x, meta_tokens, g)


def _inproj_kernel(h_ref, w_ref, ws_ref, o_ref, os_ref):
    for c in range(PROJ_TN // PROJ_SUB):
        cols = slice(c * PROJ_SUB, (c + 1) * PROJ_SUB)
        o_ref[:, cols] = jnp.dot(h_ref[...], w_ref[:, cols], preferred_element_type=F32).astype(o_ref.dtype)

    @pl.when(pl.program_id(1) == 0)
    def _():
        os_ref[...] = jnp.dot(h_ref[...], ws_ref[...], preferred_element_type=F32)


def _inproj(hn, w_big, w_small):
    T, d = hn.shape
    return pl.pallas_call(
        _inproj_kernel,
        grid=(T // PROJ_TM, N_BIG // PROJ_TN),
        in_specs=[pl.BlockSpec((PROJ_TM, d), lambda i, j: (i, 0)),
                  pl.BlockSpec((d, PROJ_TN), lambda i, j: (0, j)),
                  pl.BlockSpec((d, N_SMALL), lambda i, j: (0, 0))],
        out_specs=[pl.BlockSpec((PROJ_TM, PROJ_TN), lambda i, j: (i, j)),
                   pl.BlockSpec((PROJ_TM, N_SMALL), lambda i, j: (i, 0))],
        out_shape=[jax.ShapeDtypeStruct((T, N_BIG), BF16),
                   jax.ShapeDtypeStruct((T, N_SMALL), F32)],
        compiler_params=pltpu.CompilerParams(
            dimension_semantics=("arbitrary", "arbitrary"), vmem_limit_bytes=VMEM_LIMIT),
        name="in_proj",
    )(hn, w_big, w_small)


def _gla_units(q_ref, k_ref, v_ref, r_ref, sm_ref, wg_ref, bg_ref, nw_ref, o_ref, st_ref):
    logits = jnp.dot(sm_ref[0].astype(BF16), wg_ref[...], preferred_element_type=F32) + bg_ref[...]
    gk_all = _log_sigmoid(logits) * (1.0 / GLA_NORMALIZER)
    causal = _tril_ones(GLA_CHUNK)
    tril = causal.astype(BF16)

    def unit(rows, g_all, h):
        hk = slice(h * GLA_HK, (h + 1) * GLA_HK)
        hv = slice(h * GLA_HV, (h + 1) * GLA_HV)
        g = g_all[:, hk]
        g_last = g[GLA_CHUNK - 1:GLA_CHUNK, :]
        q = q_ref[0, rows, hk].astype(F32) * (GLA_HK ** -0.5)
        k = k_ref[0, rows, hk].astype(F32)
        v = v_ref[0, rows, hv]
        q_dec = (q * jnp.exp(g)).astype(BF16)
        k_inv = (k * jnp.exp(-g)).astype(BF16)
        k_end = (k * jnp.exp(g_last - g)).astype(BF16)
        attn = jnp.where(causal, _nt(q_dec, k_inv), 0.0).astype(BF16)
        state_t = st_ref[h]
        o = jnp.dot(attn, v, preferred_element_type=F32) + _nt(q_dec, state_t.astype(BF16))
        st_ref[h] = state_t * jnp.exp(g_last) + _tn(v, k_end)
        o = o * lax.rsqrt(jnp.mean(o * o, axis=-1, keepdims=True) + EPS) * nw_ref[...]
        o = o * _silu(r_ref[0, rows, hv].astype(F32))
        o_ref[0, rows, hv] = o.astype(o_ref.dtype)

    units = []
    for c in range(GLA_ROWS // GLA_CHUNK):
        rows = slice(c * GLA_CHUNK, (c + 1) * GLA_CHUNK)
        g_all = _split_dot(tril, gk_all[rows])
        units += [functools.partial(unit, rows, g_all, h) for h in range(GLA_HEADS)]
    return units


def _swa_units(n, q_ref, kp_ref, kc_ref, vp_ref, vc_ref, sink_ref, o_ref):
    nk = 2 * SWA_BLOCK
    low_q = lax.broadcasted_iota(jnp.int32, (SWA_BLOCK, LANES), 1) < SWA_HD
    low_k = lax.broadcasted_iota(jnp.int32, (nk, LANES), 1) < SWA_HD

    both_k = jnp.concatenate([kp_ref[0], kc_ref[0]], axis=0).astype(F32)
    swapped_k = pltpu.roll(both_k, SWA_HD, axis=1)
    k_ext = (jnp.where(low_k, both_k, swapped_k).astype(BF16), jnp.where(low_k, swapped_k, both_k).astype(BF16))
    both_v = jnp.concatenate([vp_ref[0], vc_ref[0]], axis=0).astype(F32)
    swapped_v = pltpu.roll(both_v, SWA_HD, axis=1)
    v_ones = []
    for in_low, in_high in ((both_v, swapped_v), (swapped_v, both_v)):
        v_ones.append((jnp.where(low_k, in_low, 1.0).astype(BF16),
                       jnp.where(low_k, 1.0, in_high).astype(BF16)))
    qi = lax.broadcasted_iota(jnp.int32, (SWA_BLOCK, nk), 0)
    kk = lax.broadcasted_iota(jnp.int32, (SWA_BLOCK, nk), 1)
    allowed = (kk > qi) & (kk <= qi + SWA_BLOCK) & (kk >= PAD + SWA_BLOCK - SWA_BLOCK * n)
    heads_per_kv = SWA_HQ // SWA_HKV

    def unit(j):
        hkv = (2 * j) // heads_per_kv
        cols = slice(j * LANES, (j + 1) * LANES)
        q_pair = q_ref[0, :, cols] * (SWA_HD ** -0.5)
        full, sink_term = [], []
        for half in range(2):
            q_half = jnp.where(low_q if half == 0 else ~low_q, q_pair, jnp.zeros_like(q_pair))
            s = jnp.where(allowed, _nt(q_half, k_ext[hkv]), -jnp.inf)
            sink = sink_ref[2 * j + half]
            m = jnp.maximum(jnp.max(s, axis=-1, keepdims=True), sink)
            p = jnp.exp(s - m).astype(BF16)
            full.append(jnp.dot(p, v_ones[hkv][half], preferred_element_type=F32))
            sink_term.append(jnp.exp(sink - m))
        num = jnp.where(low_q, full[0], full[1])
        sums = pltpu.roll(jnp.where(low_q, full[1], full[0]), SWA_HD, axis=1)
        den = sums + jnp.where(low_q, sink_term[0], sink_term[1])
        o_ref[0, :, cols] = (num * (1.0 / den)).astype(o_ref.dtype)

    return [functools.partial(unit, j) for j in range(SWA_HQ // 2)]


def _mix_kernel(gq_ref, gk_ref, gv_ref, gr_ref, sm_ref, wg_ref, bg_ref, nw_ref,
                sq_ref, kp_ref, kc_ref, vp_ref, vc_ref, sink_ref,
                gla_ref, swa_ref, st_ref):
    n = pl.program_id(1)

    @pl.when(n == 0)
    def _():
        st_ref[...] = jnp.zeros_like(st_ref)

    gla = _gla_units(gq_ref, gk_ref, gv_ref, gr_ref, sm_ref, wg_ref, bg_ref, nw_ref, gla_ref, st_ref)
    swa = _swa_units(n, sq_ref, kp_ref, kc_ref, vp_ref, vc_ref, sink_ref, swa_ref)
    for unit in gla + swa:
        unit()


def _mix(proj, small, wg, bg, nw, sinks):
    bsz, L, _ = proj.shape
    assert GLA_ROWS == SWA_BLOCK
    kvw = SWA_HKV * SWA_HD
    cur = lambda seg: (lambda b, n: (b, n, _seg_block(seg)))
    prev = lambda seg: (lambda b, n: (b, jnp.maximum(n - 1, 0), _seg_block(seg)))
    const = lambda b, n: (0, 0)
    rows = lambda b, n: (b, n, 0)
    return pl.pallas_call(
        _mix_kernel,
        grid=(bsz, L // GLA_ROWS),
        in_specs=[pl.BlockSpec((1, GLA_ROWS, GLA_DK), cur(SEG_GQ)),
                  pl.BlockSpec((1, GLA_ROWS, GLA_DK), cur(SEG_GK)),
                  pl.BlockSpec((1, GLA_ROWS, GLA_DV), cur(SEG_GV)),
                  pl.BlockSpec((1, GLA_ROWS, GLA_DV), cur(SEG_GR)),
                  pl.BlockSpec((1, GLA_ROWS, N_SMALL), rows),
                  pl.BlockSpec((N_SMALL, GLA_DK), const),
                  pl.BlockSpec((1, GLA_DK), const),
                  pl.BlockSpec((1, GLA_HV), const),
                  pl.BlockSpec((1, SWA_BLOCK, SWA_HQ * SWA_HD), cur(SEG_SQ)),
                  pl.BlockSpec((1, SWA_BLOCK, kvw), prev(SEG_SK)),
                  pl.BlockSpec((1, SWA_BLOCK, kvw), cur(SEG_SK)),
                  pl.BlockSpec((1, SWA_BLOCK, kvw), prev(SEG_SV)),
                  pl.BlockSpec((1, SWA_BLOCK, kvw), cur(SEG_SV)),
                  pl.BlockSpec(memory_space=pltpu.SMEM)],
        out_specs=[pl.BlockSpec((1, GLA_ROWS, GLA_DV), rows),
                   pl.BlockSpec((1, SWA_BLOCK, SWA_HQ * SWA_HD), rows)],
        out_shape=[jax.ShapeDtypeStruct((bsz, L, GLA_DV), BF16),
                   jax.ShapeDtypeStruct((bsz, L, SWA_HQ * SWA_HD), BF16)],
        scratch_shapes=[pltpu.VMEM((GLA_HEADS, GLA_HV, GLA_HK), F32)],
        compiler_params=pltpu.CompilerParams(
            dimension_semantics=("arbitrary", "arbitrary"), vmem_limit_bytes=VMEM_LIMIT),
        name="gla_swa",
    )(proj, proj, proj, proj, small, wg, bg, nw, proj, proj, proj, proj, proj, sinks)


def _ssd_block(n, z_ref, xbc_ref, sm_ref, cw_ref, cb_ref, dtb_ref, alog_ref, dskip_ref, nw_ref, shift_ref,
               o_ref, ext_ref, xs_ref, y_ref, st_ref, after_group):
    rows = SSD_CHUNK

    ext_ref[rows:2 * rows, :] = xbc_ref[0]
    col_chunk = 512
    bc_parts = []
    for c in range(SSD_XBC // col_chunk):
        cols = slice(c * col_chunk, (c + 1) * col_chunk)
        shifted = jnp.dot(shift_ref[...], ext_ref[:, cols], preferred_element_type=F32)
        acc = cb_ref[:, cols] + cw_ref[SSD_CONV - 1:SSD_CONV, cols] * xbc_ref[0, :, cols].astype(F32)
        for back in range(1, SSD_CONV):
            tap = SSD_CONV - 1 - back
            acc = acc + cw_ref[tap:tap + 1, cols] * shifted[(back - 1) * rows:back * rows, :]
        act = _silu(acc)
        if c < SSD_DINNER // col_chunk:
            xs_ref[:, cols] = act
        else:
            bc_parts.append(act.astype(BF16))
    ext_ref[0:rows, :] = xbc_ref[0]
    b_all, c_all = bc_parts

    lane = lax.broadcasted_iota(jnp.int32, (1, LANES), 1)
    pos = n * rows + lax.broadcasted_iota(jnp.int32, (rows, 1), 0)
    valid = (pos >= PAD).astype(F32)
    dt = _softplus(sm_ref[0] + dtb_ref[...]) * valid
    a_coef = jnp.where(lane < SSD_HEADS, -jnp.exp(alog_ref[...]), 0.0)
    causal = _tril_ones(rows)
    a_cs = _split_dot(causal.astype(BF16), dt * a_coef)
    a_cs_t = a_cs.T
    a_last_t = a_cs_t[:, rows - 1:rows]
    dt_t = dt.T
    w_end_t = dt_t * jnp.exp(a_last_t - a_cs_t)
    c_decay_t = jnp.exp(a_last_t)

    low = lax.broadcasted_iota(jnp.int32, (rows, LANES), 1) < SSD_HEADDIM
    top = lax.broadcasted_iota(jnp.int32, (LANES, 1), 0) < SSD_HEADDIM

    def col(x, h):
        return jnp.broadcast_to(x[:, h:h + 1], (rows, LANES))

    pairs_per_group = SSD_GROUP_WIDTH // LANES
    for g in range(SSD_GROUPS):
        gs = slice(g * SSD_DSTATE, (g + 1) * SSD_DSTATE)
        b_g = b_all[:, gs]
        c_g = c_all[:, gs]
        cb = _nt(c_g, b_g)
        srows = slice(g * SSD_GROUP_WIDTH, (g + 1) * SSD_GROUP_WIDTH)
        y_off = _nt(c_g, st_ref[srows, :].astype(BF16))
        ssq = jnp.zeros((rows, 1), F32)
        x_end_t, decay = [], []
        for jj in range(pairs_per_group):
            j = g * pairs_per_group + jj
            ha, hb = 2 * j, 2 * j + 1
            cols = slice(j * LANES, (j + 1) * LANES)
            xs = xs_ref[:, cols]
            xs_bf = xs.astype(BF16)
            a_col = (col(a_cs, ha), col(a_cs, hb))
            m_pair = []
            for h, a_c in ((ha, a_col[0]), (hb, a_col[1])):
                seg = a_c - a_cs_t[h:h + 1, :]
                m_h = cb * jnp.exp(jnp.where(causal, seg, -jnp.inf)) * dt_t[h:h + 1, :]
                m_pair.append(m_h.astype(BF16))
            zero = jnp.zeros_like(xs_bf)
            xs_split = jnp.concatenate([jnp.where(low, xs_bf, zero), jnp.where(low, zero, xs_bf)], axis=0)
            y = jnp.dot(jnp.concatenate(m_pair, axis=1), xs_split, preferred_element_type=F32)
            y = y + y_off[:, jj * LANES:(jj + 1) * LANES] * jnp.exp(jnp.where(low, a_col[0], a_col[1]))
            y = y + dskip_ref[:, cols] * xs
            w_rows = jnp.where(top, w_end_t[ha:ha + 1, :], w_end_t[hb:hb + 1, :])
            x_end_t.append((xs.T * w_rows).astype(BF16))
            decay.append(jnp.where(top, c_decay_t[ha:ha + 1, :], c_decay_t[hb:hb + 1, :]))
            y = y * _silu(z_ref[0, :, cols].astype(F32))
            ssq = ssq + jnp.sum(y * y, axis=-1, keepdims=True)
            y_ref[:, cols] = y
        st_ref[srows, :] = (st_ref[srows, :] * jnp.concatenate(decay, axis=0)
                            + jnp.dot(jnp.concatenate(x_end_t, axis=0), b_g, preferred_element_type=F32))
        scale = lax.rsqrt(ssq * (1.0 / SSD_GROUP_WIDTH) + EPS)
        gcols = slice(g * SSD_GROUP_WIDTH, (g + 1) * SSD_GROUP_WIDTH)
        o_ref[:, gcols] = (y_ref[:, gcols] * scale * nw_ref[:, gcols]).astype(o_ref.dtype)
        after_group(g)


def _merge_block(j, ga_ref, sw_ref, sd_ref, gl_ref, x_ref, wa_ref, wb_ref, wc_ref, wo_ref, bg_ref):
    rows = SSD_CHUNK
    pos = j * rows + lax.broadcasted_iota(jnp.int32, (rows, 1), 0)
    valid = (pos >= PAD).astype(F32)
    gates = jax.nn.sigmoid(gl_ref[0].astype(F32) + bg_ref[...])
    merged = gates[:, 0:D_MODEL] * jnp.dot(ga_ref[0], wa_ref[...], preferred_element_type=F32)
    merged += gates[:, D_MODEL:2 * D_MODEL] * jnp.dot(sw_ref[0], wb_ref[...], preferred_element_type=F32)
    merged += gates[:, 2 * D_MODEL:] * jnp.dot(sd_ref[...], wc_ref[...], preferred_element_type=F32)
    x = x_ref[0] + jnp.dot(merged.astype(BF16), wo_ref[...], preferred_element_type=F32) * valid
    return x, valid


def _mlp_chunk(c, h2, w1_ref, w2_ref):
    ff_chunk = D_FF // SSD_GROUPS
    cols = slice(c * ff_chunk, (c + 1) * ff_chunk)
    hid = jnp.dot(h2, w1_ref[:, cols], preferred_element_type=F32)
    hid = jnp.square(jnp.maximum(hid, 0.0)).astype(BF16)
    return jnp.dot(hid, w2_ref[cols, :], preferred_element_type=F32)


def _tail_kernel(z_ref, xbc_ref, sm_ref, cw_ref, cb_ref, dtb_ref, alog_ref, dskip_ref, nw_ref, shift_ref,
                 ga_ref, sw_ref, gl_ref, x_ref, wa_ref, wb_ref, wc_ref, wo_ref, bg_ref,
                 nm_ref, w1_ref, w2_ref, gn_ref, *outs_and_scratch, emit_stream):
    if emit_stream:
        xo_ref, hn_ref, ext_ref, xs_ref, y_ref, st_ref, cur_ref, prev_ref = outs_and_scratch
    else:
        xo_ref = None
        hn_ref, ext_ref, xs_ref, y_ref, st_ref, cur_ref, prev_ref = outs_and_scratch
    n = pl.program_id(1)
    last = pl.num_programs(1) - 1

    def ssd(after_group):
        _ssd_block(n, z_ref, xbc_ref, sm_ref, cw_ref, cb_ref, dtb_ref, alog_ref, dskip_ref, nw_ref, shift_ref,
                   cur_ref, ext_ref, xs_ref, y_ref, st_ref, after_group=after_group)

    def merge_mlp(run_ssd):
        prev_ref[...] = cur_ref[...]
        x, valid = _merge_block(n - 1, ga_ref, sw_ref, prev_ref, gl_ref, x_ref, wa_ref, wb_ref, wc_ref,
                                wo_ref, bg_ref)
        h2 = _rms(x, nm_ref[...]).astype(BF16)
        mlp_out = [jnp.zeros_like(x)]

        def mlp_chunk(c):
            mlp_out[0] = mlp_out[0] + _mlp_chunk(c, h2, w1_ref, w2_ref)

        if run_ssd:
            ssd(mlp_chunk)
        else:
            for c in range(SSD_GROUPS):
                mlp_chunk(c)
        xo = x + mlp_out[0] * valid
        if emit_stream:
            xo_ref[0] = xo
        hn_ref[0] = _rms(xo, gn_ref[...]).astype(hn_ref.dtype)

    @pl.when(n == 0)
    def _():
        ext_ref[0:SSD_CHUNK, :] = jnp.zeros((SSD_CHUNK, SSD_XBC), BF16)
        st_ref[...] = jnp.zeros_like(st_ref)
        hn_ref[...] = jnp.zeros_like(hn_ref)
        if emit_stream:
            xo_ref[...] = jnp.zeros_like(xo_ref)
        ssd(lambda g: None)

    @pl.when((n > 0) & (n < last))
    def _():
        merge_mlp(run_ssd=True)

    @pl.when(n == last)
    def _():
        merge_mlp(run_ssd=False)


def _conv_shift_matrix():
    r = jnp.arange((SSD_CONV - 1) * SSD_CHUNK)
    source = SSD_CHUNK + r % SSD_CHUNK - (r // SSD_CHUNK + 1)
    return (jnp.arange(2 * SSD_CHUNK)[None, :] == source[:, None]).astype(BF16)


def _tail(proj, small, y_gla, y_swa, x, ssd_params, merge_params, mlp_params, final):
    bsz, L, d = x.shape
    nb = L // SSD_CHUNK
    assert LEAD == SSD_CHUNK
    cur = lambda seg: (lambda b, n: (b, jnp.minimum(n, nb - 1), seg))
    lag = lambda seg: (lambda b, n: (b, jnp.maximum(n - 1, 0), seg))
    if final:
        out_specs = [pl.BlockSpec((1, SSD_CHUNK, d), lambda b, n: (b, jnp.maximum(n - 2, 0), 0))]
        out_shape = [jax.ShapeDtypeStruct((bsz, L - LEAD, d), F32)]
    else:
        out_specs = [pl.BlockSpec((1, SSD_CHUNK, d), lag(0)), pl.BlockSpec((1, SSD_CHUNK, d), lag(0))]
        out_shape = [jax.ShapeDtypeStruct((bsz, L, d), F32), jax.ShapeDtypeStruct((bsz, L, d), BF16)]
    const = lambda b, n: (0, 0)
    single = pl.Buffered(1)
    conv_w, conv_b, dt_bias, a_log, d_exp, ssd_nw = ssd_params
    wa, wb, wc, wo, b_gate = merge_params
    norm_w, w1, w2, next_norm_w = mlp_params
    return pl.pallas_call(
        functools.partial(_tail_kernel, emit_stream=not final),
        grid=(bsz, nb + 1),
        in_specs=[pl.BlockSpec((1, SSD_CHUNK, SSD_DINNER), cur(_seg_block(SEG_Z))),
                  pl.BlockSpec((1, SSD_CHUNK, SSD_XBC), cur(_seg_block(SEG_XBC))),
                  pl.BlockSpec((1, SSD_CHUNK, N_SMALL), cur(0)),
                  pl.BlockSpec((SSD_CONV, SSD_XBC), const),
                  pl.BlockSpec((1, SSD_XBC), const),
                  pl.BlockSpec((1, N_SMALL), const),
                  pl.BlockSpec((1, N_SMALL), const),
                  pl.BlockSpec((1, SSD_DINNER), const),
                  pl.BlockSpec((1, SSD_DINNER), const),
                  pl.BlockSpec(((SSD_CONV - 1) * SSD_CHUNK, 2 * SSD_CHUNK), const),
                  pl.BlockSpec((1, SSD_CHUNK, GLA_DV), lag(0)),
                  pl.BlockSpec((1, SSD_CHUNK, SWA_HQ * SWA_HD), lag(0)),
                  pl.BlockSpec((1, SSD_CHUNK, N_BRANCH * D_MODEL), lag(_seg_block(SEG_GATE))),
                  pl.BlockSpec((1, SSD_CHUNK, d), lag(0)),
                  pl.BlockSpec((GLA_DV, d), const, pipeline_mode=single),
                  pl.BlockSpec((SWA_HQ * SWA_HD, d), const, pipeline_mode=single),
                  pl.BlockSpec((SSD_DINNER, d), const, pipeline_mode=single),
                  pl.BlockSpec((d, d), const, pipeline_mode=single),
                  pl.BlockSpec((1, N_BRANCH * D_MODEL), const),
                  pl.BlockSpec((1, d), const),
                  pl.BlockSpec((d, D_FF), const, pipeline_mode=single),
                  pl.BlockSpec((D_FF, d), const, pipeline_mode=single),
                  pl.BlockSpec((1, d), const)],
        out_specs=out_specs,
        out_shape=out_shape,
        scratch_shapes=[pltpu.VMEM((2 * SSD_CHUNK, SSD_XBC), BF16),
                        pltpu.VMEM((SSD_CHUNK, SSD_DINNER), F32),
                        pltpu.VMEM((SSD_CHUNK, SSD_DINNER), F32),
                        pltpu.VMEM((SSD_DINNER, SSD_DSTATE), F32),
                        pltpu.VMEM((SSD_CHUNK, SSD_DINNER), BF16),
                        pltpu.VMEM((SSD_CHUNK, SSD_DINNER), BF16)],
        compiler_params=pltpu.CompilerParams(
            dimension_semantics=("arbitrary", "arbitrary"), vmem_limit_bytes=VMEM_LIMIT),
        name="ssd_merge_mlp",
    )(proj, proj, small, conv_w, conv_b, dt_bias, a_log, d_exp, ssd_nw, _conv_shift_matrix(),
      y_gla, y_swa, proj, x, wa, wb, wc, wo, b_gate, norm_w, w1, w2, next_norm_w)


def _split_w_in(w_in):
    sizes = (GLA_DK, GLA_DK, GLA_DV, GLA_DV, GLA_RANK, SWA_HQ * SWA_HD, SWA_HKV * SWA_HD,
             SWA_HKV * SWA_HD, SSD_DINNER, SSD_XBC, SSD_HEADS, N_BRANCH * D_MODEL)
    starts = [sum(sizes[:i]) for i in range(len(sizes))]
    part = lambda i: w_in[:, :, starts[i]:starts[i] + sizes[i]]
    g_q, g_k, g_v, g_r, g_lr, s_q, s_k, s_v, m_z, m_xbc, m_dt, gate = (part(i) for i in range(len(sizes)))
    big = jnp.concatenate([gate, m_xbc, m_z, g_v, g_r, s_q, g_q, g_k, s_k, s_v], axis=-1).astype(BF16)
    pad = jnp.zeros(w_in.shape[:2] + (N_SMALL - SSD_HEADS - GLA_RANK,), w_in.dtype)
    small = jnp.concatenate([m_dt, g_lr, pad], axis=-1).astype(BF16)
    return big, small


def _pad_lanes(v, offset=0):
    out = jnp.zeros((v.shape[0], 1, LANES), F32)
    return out.at[:, 0, offset:offset + v.shape[1]].set(v.astype(F32))


def kernel(x, meta_tokens, norm_mix, w_in, b_gate, gla_w_gate_up, gla_b_gate_up, gla_norm, swa_sinks,
           ssd_conv_w, ssd_conv_b, ssd_dt_bias, ssd_A_log, ssd_D, ssd_norm,
           w_branch_gla, w_branch_swa, w_branch_ssd, w_out, norm_mlp, w_mlp_in, w_mlp_out, final_norm):
    bsz, seq, d = x.shape
    L = LEAD + seq

    w_big, w_small = _split_w_in(w_in)
    wg = jnp.zeros((DEPTH, N_SMALL, GLA_DK), F32).at[:, SMALL_LR:SMALL_LR + GLA_RANK, :].set(gla_w_gate_up)
    wg = wg.astype(BF16)
    dt_bias = _pad_lanes(ssd_dt_bias, SMALL_DT)
    a_log = _pad_lanes(ssd_A_log, SMALL_DT)
    d_exp = jnp.repeat(ssd_D.astype(F32), SSD_HEADDIM, axis=-1)[:, None, :]
    wa, wb, wc, wo = (w.astype(BF16) for w in (w_branch_gla, w_branch_swa, w_branch_ssd, w_out))
    w1, w2 = w_mlp_in.astype(BF16), w_mlp_out.astype(BF16)
    row = lambda p: p.astype(F32)[:, None, :]
    norm_mix_r, norm_mlp_r, b_gate_r = row(norm_mix), row(norm_mlp), row(b_gate)
    gla_bg_r, gla_nw_r = row(gla_b_gate_up), row(gla_norm)
    conv_b_r, ssd_nw_r = row(ssd_conv_b), row(ssd_norm)
    final_r = final_norm.astype(F32)[None, :]

    h, hn = _embed(x.astype(F32), meta_tokens.astype(F32), norm_mix_r[0])
    for i in range(DEPTH):
        proj, small = _inproj(hn.reshape(bsz * L, d), w_big[i], w_small[i])
        proj = proj.reshape(bsz, L, N_BIG)
        small = small.reshape(bsz, L, N_SMALL)
        y_gla, y_swa = _mix(proj, small, wg[i], gla_bg_r[i], gla_nw_r[i], swa_sinks[i].astype(F32))
        last = i == DEPTH - 1
        next_g = final_r if last else norm_mix_r[i + 1]
        outs = _tail(proj, small, y_gla, y_swa, h,
                     (ssd_conv_w[i].astype(F32), conv_b_r[i], dt_bias[i], a_log[i], d_exp[i], ssd_nw_r[i]),
                     (wa[i], wb[i], wc[i], wo[i], b_gate_r[i]),
                     (norm_mlp_r[i], w1[i], w2[i], next_g), final=last)
        if last:
            return outs[0]
        h, hn = outs
```
